```python
import math
import jax, jax.numpy as jnp
from jax import lax
import numpy as np

D_MODEL = 1024
BATCH = 16
SEQ = 2048
DEPTH = 4

MLA_HEADS = 6
MLA_NOPE = 64
MLA_ROPE = 32
MLA_V = 64
MLA_QK = MLA_NOPE + MLA_ROPE
MLA_Q_RANK = 192
MLA_KV_RANK = 128
DIL_HEADS = 6
DIL_HEAD_DIM = 64
DIL_PATTERNS = ((128, 1), (512, 4), (2048, 16))
SSM_GROUPS = 16
SSM_GROUP_CH = 16
SSM_CH = SSM_GROUPS * SSM_GROUP_CH
SSM_STATE = 64
DT_MIN = 1e-3
DT_MAX = 1e-1
WIDTH_A = MLA_HEADS * MLA_V
WIDTH_B = DIL_HEADS * DIL_HEAD_DIM
MIX_WIDTH = WIDTH_A + WIDTH_B + SSM_CH
IN_A = MLA_Q_RANK + MLA_KV_RANK + MLA_ROPE
IN_B = 3 * WIDTH_B
IN_C = SSM_CH
IN_WIDTH = IN_A + IN_B + IN_C
FFN_HIDDEN = 2816
CONV_WIDTH = 3
ROPE_THETA = 10000.0
EPS = 1e-6
Q_BLOCK = 128
NEG_INF = -1e30
N_MOD = 6

kernel_name = "hymba_style_mla_dilated_s5_convffn_encoder"


def rmsnorm(x, gain):
    xf = x.astype(jnp.float32)
    y = xf * lax.rsqrt(jnp.mean(xf * xf, axis=-1, keepdims=True) + EPS)
    return (y * gain.astype(jnp.float32)).astype(x.dtype)


def rope_tables(positions, dim):
    inv_freq = 1.0 / (ROPE_THETA ** (jnp.arange(0, dim, 2, dtype=jnp.float32) / dim))
    ang = positions.astype(jnp.float32)[..., None] * inv_freq
    return jnp.cos(ang), jnp.sin(ang)


def apply_rope(t, cos, sin):
    tf = t.astype(jnp.float32)
    t1, t2 = jnp.split(tf, 2, axis=-1)
    cs, sn = cos[:, None], sin[:, None]
    return jnp.concatenate([t1 * cs - t2 * sn, t1 * sn + t2 * cs], axis=-1).astype(t.dtype)


def split_heads(t, n_heads):
    b, s, _ = t.shape
    return t.reshape(b, s, n_heads, -1).transpose(0, 2, 1, 3)


def merge_heads(t):
    b, h, s, e = t.shape
    return t.transpose(0, 2, 1, 3).reshape(b, s, h * e)


def dense_attention(q, k, v):
    b, h, s, e = q.shape
    nb = s // Q_BLOCK
    scale = e ** -0.5
    vf = v.astype(jnp.float32)
    q_blocks = q.reshape(b, h, nb, Q_BLOCK, e).transpose(2, 0, 1, 3, 4)

    def attend(qb):
        sc = jnp.einsum("bhqe,bhke->bhqk", qb, k, preferred_element_type=jnp.float32) * scale
        p = jax.nn.softmax(sc, axis=-1)
        return jnp.einsum("bhqk,bhkd->bhqd", p, vf)

    out = lax.map(attend, q_blocks)
    return out.transpose(1, 2, 0, 3, 4).reshape(b, h, s, v.shape[-1]).astype(v.dtype)


def mla_mixer(u, cos_r, sin_r, q_gain, w_uq, kv_gain, w_ukv, qk_gain):
    c_q, c_kv, k_rope = jnp.split(u, [MLA_Q_RANK, MLA_Q_RANK + MLA_KV_RANK], axis=-1)
    q = split_heads(rmsnorm(c_q, q_gain) @ w_uq, MLA_HEADS)
    kv = split_heads(rmsnorm(c_kv, kv_gain) @ w_ukv, MLA_HEADS)
    k_nope, v = jnp.split(kv, [MLA_NOPE], axis=-1)
    k_rope = jnp.broadcast_to(k_rope[:, None], k_nope.shape[:-1] + (MLA_ROPE,))
    q = rmsnorm(q, qk_gain[0])
    k = rmsnorm(jnp.concatenate([k_nope, k_rope], axis=-1), qk_gain[1])
    q = jnp.concatenate([q[..., :MLA_NOPE], apply_rope(q[..., MLA_NOPE:], cos_r, sin_r)], axis=-1)
    k = jnp.concatenate([k[..., :MLA_NOPE], apply_rope(k[..., MLA_NOPE:], cos_r, sin_r)], axis=-1)
    return merge_heads(dense_attention(q, k, v))


def dilated_branch(q, k, v, window, dilation):
    b, h, s, e = q.shape
    half = window // (2 * dilation)
    blk = half
    n_sub = s // dilation
    nb = -(-n_sub // blk)
    lp = nb * blk

    def by_residue(t):
        return t.reshape(b, h, n_sub, dilation, e).transpose(0, 1, 3, 2, 4)

    qb = jnp.pad(by_residue(q), ((0, 0), (0, 0), (0, 0), (0, lp - n_sub), (0, 0)))
    qb = qb.reshape(b, h, dilation, nb, blk, e)

    def band(t):
        tp = jnp.pad(by_residue(t), ((0, 0), (0, 0), (0, 0), (blk, lp - n_sub + blk), (0, 0)))
        return jnp.concatenate(
            [tp[:, :, :, i * blk:i * blk + lp].reshape(b, h, dilation, nb, blk, e) for i in range(3)],
            axis=-2)

    kb, vb = band(k), band(v)
    sc = jnp.einsum("bhrnqe,bhrnke->bhrnqk", qb, kb, preferred_element_type=jnp.float32) * (e ** -0.5)
    q_idx = jnp.arange(lp).reshape(nb, blk, 1)
    k_idx = (jnp.arange(nb)[:, None, None] - 1) * blk + jnp.arange(3 * blk)[None, None, :]
    valid = (jnp.abs(k_idx - q_idx) <= half) & (k_idx >= 0) & (k_idx < n_sub)
    sc = jnp.where(valid, sc, NEG_INF)
    m = jnp.max(sc, axis=-1, keepdims=True)
    p = jnp.exp(sc - m)
    den = jnp.sum(p, axis=-1, keepdims=True)
    o = jnp.einsum("bhrnqk,bhrnke->bhrnqe", p, vb.astype(jnp.float32)) / den
    lse = (m + jnp.log(den))[..., 0]

    def back(t):
        t = t.reshape((b, h, dilation, lp) + t.shape[5:])[:, :, :, :n_sub]
        t = jnp.moveaxis(t, 2, 3)
        return t.reshape((b, h, s) + t.shape[4:])

    return back(o), back(lse)


def dilated_mixer(u, cos_f, sin_f, qk_gain):
    q, k, v = (split_heads(t, DIL_HEADS) for t in jnp.split(u, 3, axis=-1))
    q = apply_rope(rmsnorm(q, qk_gain[0]), cos_f, sin_f)
    k = apply_rope(rmsnorm(k, qk_gain[1]), cos_f, sin_f)
    branches = [dilated_branch(q, k, v, w, d) for w, d in DIL_PATTERNS]
    outs = jnp.stack([o for o, _ in branches])
    wts = jax.nn.softmax(jnp.stack([l for _, l in branches]), axis=0)
    out = jnp.einsum("pbhs,pbhse->bhse", wts, outs)
    return merge_heads(out).astype(u.dtype)


def s5_scan(uf, a_re, a_im, log_dt, b_re, b_im, c_re, c_im, reverse):
    a = lax.complex(a_re.astype(jnp.float32), a_im.astype(jnp.float32))
    dt = jnp.exp(log_dt.astype(jnp.float32))[:, None]
    a_bar = jnp.exp(a * dt)
    b_bar = ((a_bar - 1.0) / a)[..., None] * lax.complex(b_re.astype(jnp.float32), b_im.astype(jnp.float32))
    bu = jnp.einsum("bsgc,gpc->bsgp", uf.astype(jnp.complex64), b_bar)
    a_seq = jnp.broadcast_to(a_bar, bu.shape)

    def combine(left, right):
        a_l, b_l = left
        a_r, b_r = right
        return a_r * a_l, a_r * b_l + b_r

    _, states = lax.associative_scan(combine, (a_seq, bu), axis=1, reverse=reverse)
    c = lax.complex(c_re.astype(jnp.float32), c_im.astype(jnp.float32))
    return jnp.einsum("bsgp,gcp->bsgc", states, c).real


def s5_mixer(u, a_re, a_im, log_dt, b_re, b_im, c_re, c_im, d_skip, w_glu, b_glu):
    b, s, _ = u.shape
    uf = u.astype(jnp.float32).reshape(b, s, SSM_GROUPS, SSM_GROUP_CH)
    y_fwd = s5_scan(uf, a_re[0], a_im[0], log_dt[0], b_re[0], b_im[0], c_re[0], c_im[0], False)
    y_bwd = s5_scan(uf, a_re[1], a_im[1], log_dt[1], b_re[1], b_im[1], c_re[1], c_im[1], True)
    y = (y_fwd + y_bwd).reshape(b, s, SSM_CH) + d_skip.astype(jnp.float32) * uf.reshape(b, s, SSM_CH)
    y = jax.nn.gelu(y).astype(u.dtype)
    val, gate = jnp.split(y @ w_glu + b_glu, 2, axis=-1)
    return val * jax.nn.sigmoid(gate)


def conv_ffn(h, w_up, conv_w, conv_b, w_down):
    z = h @ w_up
    z = lax.conv_general_dilated(
        z, conv_w[:, None, :], window_strides=(1,),
        padding=((CONV_WIDTH // 2, CONV_WIDTH // 2),),
        dimension_numbers=("NWC", "WIO", "NWC"),
        feature_group_count=z.shape[-1]) + conv_b
    val, gate = jnp.split(z, 2, axis=-1)
    return (jax.nn.silu(gate) * val) @ w_down


def setup_inputs(seed: int = 0) -> dict:
    key = jax.random.key(seed)
    ks = iter(jax.random.split(key, 40))
    L = DEPTH

    def nrm(shape, scale):
        return jax.random.normal(next(ks), shape, jnp.float32) * scale

    def gain(shape):
        return 1.0 + nrm(shape, 0.02)

    x = nrm((BATCH, SEQ, D_MODEL), 1.0)
    c = nrm((BATCH, D_MODEL), 1.0)
    offsets = jax.random.randint(next(ks), (BATCH, 1), 0, 4096, dtype=jnp.int32)
    positions = offsets + jnp.arange(SEQ, dtype=jnp.int32)[None, :]
    n_idx = jnp.arange(SSM_STATE, dtype=jnp.float32)
    return {
        "x": x,
        "c": c,
        "positions": positions,
        "w_mod": nrm((L, D_MODEL, N_MOD * D_MODEL), 0.5 * D_MODEL ** -0.5),
        "b_mod": nrm((L, N_MOD * D_MODEL), 0.01),
        "norm1": gain((L, D_MODEL)),
        "w_in": nrm((L, D_MODEL, IN_WIDTH), D_MODEL ** -0.5),
        "mla_q_norm": gain((L, MLA_Q_RANK)),
        "mla_w_uq": nrm((L, MLA_Q_RANK, MLA_HEADS * MLA_QK), MLA_Q_RANK ** -0.5),
        "mla_kv_norm": gain((L, MLA_KV_RANK)),
        "mla_w_ukv": nrm((L, MLA_KV_RANK, MLA_HEADS * (MLA_NOPE + MLA_V)), MLA_KV_RANK ** -0.5),
        "mla_qk_gain": gain((L, 2, MLA_QK)),
        "dil_qk_gain": gain((L, 2, DIL_HEAD_DIM)),
        "ssm_a_re": -0.5 + nrm((L, 2, SSM_GROUPS, SSM_STATE), 0.01),
        "ssm_a_im": math.pi * n_idx + nrm((L, 2, SSM_GROUPS, SSM_STATE), 0.01),
        "ssm_log_dt": jax.random.uniform(next(ks), (L, 2, SSM_GROUPS), jnp.float32,
                                         math.log(DT_MIN), math.log(DT_MAX)),
        "ssm_b_re": nrm((L, 2, SSM_GROUPS, SSM_STATE, SSM_GROUP_CH), (2 * SSM_GROUP_CH) ** -0.5),
        "ssm_b_im": nrm((L, 2, SSM_GROUPS, SSM_STATE, SSM_GROUP_CH), (2 * SSM_GROUP_CH) ** -0.5),
        "ssm_c_re": nrm((L, 2, SSM_GROUPS, SSM_GROUP_CH, SSM_STATE), SSM_STATE ** -0.5),
        "ssm_c_im": nrm((L, 2, SSM_GROUPS, SSM_GROUP_CH, SSM_STATE), SSM_STATE ** -0.5),
        "ssm_d": nrm((L, SSM_CH), 1.0),
        "ssm_w_glu": nrm((L, SSM_CH, 2 * SSM_CH), SSM_CH ** -0.5),
        "ssm_b_glu": nrm((L, 2 * SSM_CH), 0.01),
        "mix_norm": gain((L, MIX_WIDTH)),
        "w_out": nrm((L, MIX_WIDTH, D_MODEL), MIX_WIDTH ** -0.5),
        "norm2": gain((L, D_MODEL)),
        "ffn_w_up": nrm((L, D_MODEL, 2 * FFN_HIDDEN), D_MODEL ** -0.5),
        "ffn_conv_w": nrm((L, CONV_WIDTH, 2 * FFN_HIDDEN), 0.2)
                      + jnp.array([0.0, 1.0, 0.0], jnp.float32)[None, :, None],
        "ffn_conv_b": nrm((L, 2 * FFN_HIDDEN), 0.01),
        "ffn_w_down": nrm((L, FFN_HIDDEN, D_MODEL), FFN_HIDDEN ** -0.5),
    }


def reference(x, c, positions, w_mod, b_mod, norm1, w_in, mla_q_norm, mla_w_uq, mla_kv_norm,
              mla_w_ukv, mla_qk_gain, dil_qk_gain, ssm_a_re, ssm_a_im, ssm_log_dt, ssm_b_re,
              ssm_b_im, ssm_c_re, ssm_c_im, ssm_d, ssm_w_glu, ssm_b_glu, mix_norm, w_out, norm2,
              ffn_w_up, ffn_conv_w, ffn_conv_b, ffn_w_down):
    cos_r, sin_r = rope_tables(positions, MLA_ROPE)
    cos_f, sin_f = rope_tables(positions, DIL_HEAD_DIM)
    c_act = jax.nn.silu(c)
    for l in range(DEPTH):
        mod = c_act @ w_mod[l] + b_mod[l]
        sh1, sc1, g1, sh2, sc2, g2 = (m[:, None, :] for m in jnp.split(mod, N_MOD, axis=-1))

        h = rmsnorm(x, norm1[l]) * (1.0 + sc1) + sh1
        u = h @ w_in[l]
        u_a, u_b, u_c = jnp.split(u, [IN_A, IN_A + IN_B], axis=-1)
        o_a = mla_mixer(u_a, cos_r, sin_r, mla_q_norm[l], mla_w_uq[l], mla_kv_norm[l],
                        mla_w_ukv[l], mla_qk_gain[l])
        o_b = dilated_mixer(u_b, cos_f, sin_f, dil_qk_gain[l])
        o_c = s5_mixer(u_c, ssm_a_re[l], ssm_a_im[l], ssm_log_dt[l], ssm_b_re[l], ssm_b_im[l],
                       ssm_c_re[l], ssm_c_im[l], ssm_d[l], ssm_w_glu[l], ssm_b_glu[l])
        g_mix = mix_norm[l]
        mixed = jnp.concatenate([
            rmsnorm(o_a, g_mix[:WIDTH_A]),
            rmsnorm(o_b, g_mix[WIDTH_A:WIDTH_A + WIDTH_B]),
            rmsnorm(o_c, g_mix[WIDTH_A + WIDTH_B:]),
        ], axis=-1)
        x = x + g1 * (mixed @ w_out[l])

        h2 = rmsnorm(x, norm2[l]) * (1.0 + sc2) + sh2
        x = x + g2 * conv_ffn(h2, ffn_w_up[l], ffn_conv_w[l], ffn_conv_b[l], ffn_w_down[l])
    return x
```

```python
import functools
import math

import jax
import jax.numpy as jnp
from jax import lax
from jax.experimental import pallas as pl
from jax.experimental.pallas import tpu as pltpu

F32 = jnp.float32
BF16 = jnp.bfloat16

D_MODEL = 1024
MLA_HEADS = 6
MLA_NOPE = 64
MLA_ROPE = 32
MLA_V = 64
MLA_QK = MLA_NOPE + MLA_ROPE
MLA_Q_RANK = 192
MLA_KV_RANK = 128
DIL_HEADS = 6
DIL_HEAD_DIM = 64
DIL_PATTERNS = ((128, 1), (512, 4), (2048, 16))
SSM_GROUPS = 16
SSM_GROUP_CH = 16
SSM_CH = SSM_GROUPS * SSM_GROUP_CH
SSM_STATE = 64
WIDTH_A = MLA_HEADS * MLA_V
WIDTH_B = DIL_HEADS * DIL_HEAD_DIM
MIX_WIDTH = WIDTH_A + WIDTH_B + SSM_CH
IN_A = MLA_Q_RANK + MLA_KV_RANK + MLA_ROPE
IN_B = 3 * WIDTH_B
FFN_HIDDEN = 2816
ROPE_THETA = 10000.0
EPS = 1e-6
NEG_INF = -1e30
N_MOD = 6

LANE = 128
SUBLANE = 8
HALF = LANE // 2
Q_RANK_PAD = 256
TOKEN_TILE = 512
ATTN_Q_TILE = 512
FFN_CHUNK = 256
SSM_CHUNK = 32
VMEM_LIMIT = 56 * 1024 * 1024

C_CQ = 0
C_CKV = C_CQ + Q_RANK_PAD
C_KR = C_CKV + MLA_KV_RANK
C_KRS = C_KR + LANE
C_DQ = C_KRS + LANE
C_DQS = C_DQ + WIDTH_B
C_DK = C_DQS + WIDTH_B
C_DKS = C_DK + WIDTH_B
C_DV = C_DKS + WIDTH_B
C_UC = C_DV + WIDTH_B
IN_EXT = C_UC + SSM_CH
N_PAIRS = MLA_HEADS // 2
QA_W = MLA_HEADS * LANE


def _cparams(*sem):
    return pltpu.CompilerParams(dimension_semantics=sem, vmem_limit_bytes=VMEM_LIMIT)


def _resident(shape):
    nd = len(shape)
    return pl.BlockSpec(shape, lambda *_: (0,) * nd, pipeline_mode=pl.Buffered(1))


def _rms_scale(v, width):
    return lax.rsqrt(jnp.sum(v * v, axis=-1, keepdims=True) * (1.0 / width) + EPS)


def _bdot(a, b):
    return jnp.dot(a.astype(BF16), b, preferred_element_type=F32)


def _mod_kernel(c_ref, w_ref, b_ref, o_ref):
    c = c_ref[...]
    ca = c * (1.0 / (1.0 + jnp.exp(-c)))
    w = w_ref[...]
    c_hi = ca.astype(BF16)
    c_lo = (ca - c_hi.astype(F32)).astype(BF16)
    w_hi = w.astype(BF16)
    w_lo = (w - w_hi.astype(F32)).astype(BF16)
    acc = jnp.dot(c_hi, w_hi, preferred_element_type=F32)
    acc += jnp.dot(c_lo, w_hi, preferred_element_type=F32)
    acc += jnp.dot(c_hi, w_lo, preferred_element_type=F32)
    o_ref[...] = acc + b_ref[...]


def _modulation(c, w_mod, b_mod):
    depth, d, n = w_mod.shape
    bsz = c.shape[0]
    tn = 1024
    return pl.pallas_call(
        _mod_kernel,
        grid=(depth, n // tn),
        in_specs=[
            pl.BlockSpec((bsz, d), lambda l, j: (0, 0)),
            pl.BlockSpec((None, d, tn), lambda l, j: (l, 0, j)),
            pl.BlockSpec((None, 1, tn), lambda l, j: (l, 0, j)),
        ],
        out_specs=pl.BlockSpec((None, bsz, tn), lambda l, j: (l, 0, j)),
        out_shape=jax.ShapeDtypeStruct((depth, bsz, n), F32),
        compiler_params=_cparams("parallel", "parallel"),
        name="modulation",
    )(c, w_mod, b_mod.reshape(depth, 1, n))


def _inproj_kernel(x_ref, sc_ref, sh_ref, n1_ref, win_ref, qg_ref, wuq_ref, kvg_ref, wukv_ref,
                   ag_ref, bg_ref, ca_ref, sa_ref, cb_ref, sb_ref,
                   qa_ref, ka_ref, va_ref, qb_ref, kb_ref, vb_ref, uc_ref):
    x = x_ref[...]
    h = x * _rms_scale(x, D_MODEL) * n1_ref[...]
    h = h * (1.0 + sc_ref[...]) + sh_ref[...]
    u = _bdot(h, win_ref[...])

    cq = u[:, C_CQ:C_CQ + Q_RANK_PAD]
    cqn = cq * _rms_scale(cq, MLA_Q_RANK) * qg_ref[...]
    q_all = _bdot(cqn, wuq_ref[...])
    ckv = u[:, C_CKV:C_CKV + MLA_KV_RANK]
    ckvn = ckv * _rms_scale(ckv, MLA_KV_RANK) * kvg_ref[...]
    kv_all = _bdot(ckvn, wukv_ref[...])
    kr = u[:, C_KR:C_KR + LANE]
    krs = u[:, C_KRS:C_KRS + LANE]

    ca, sa = ca_ref[...], sa_ref[...]
    gq, gqs, gk, gks = ag_ref[0:1, :], ag_ref[1:2, :], ag_ref[2:3, :], ag_ref[3:4, :]
    cos_q, sin_q = ca * gq, sa * gqs
    cos_k, sin_k = ca * gk, sa * gks
    krs_rot = krs * sin_k
    a_scale = MLA_QK ** -0.5
    for hd in range(MLA_HEADS):
        q = q_all[:, hd * LANE:(hd + 1) * LANE]
        qs = q_all[:, QA_W + hd * LANE:QA_W + (hd + 1) * LANE]
        r = _rms_scale(q, MLA_QK) * a_scale
        qa_ref[:, hd * LANE:(hd + 1) * LANE] = ((q * cos_q + qs * sin_q) * r).astype(BF16)
        k = kv_all[:, hd * LANE:(hd + 1) * LANE] + kr
        r = _rms_scale(k, MLA_QK)
        ka_ref[:, hd * LANE:(hd + 1) * LANE] = ((k * cos_k + krs_rot) * r).astype(BF16)
    va_ref[...] = kv_all[:, QA_W:QA_W + WIDTH_A].astype(BF16)

    cb, sb = cb_ref[...], sb_ref[...]
    gq, gqs, gk, gks = bg_ref[0:1, :], bg_ref[1:2, :], bg_ref[2:3, :], bg_ref[3:4, :]
    cos_q, sin_q = cb * gq, sb * gqs
    cos_k, sin_k = cb * gk, sb * gks
    low = lax.broadcasted_iota(jnp.int32, (x.shape[0], LANE), 1) < HALF
    b_scale = DIL_HEAD_DIM ** -0.5

    def pair_scale(v):
        sq = v * v
        s_all = jnp.sum(sq, axis=-1, keepdims=True)
        s_low = jnp.sum(jnp.where(low, sq, 0.0), axis=-1, keepdims=True)
        r_low = lax.rsqrt(s_low * (1.0 / DIL_HEAD_DIM) + EPS)
        r_high = lax.rsqrt((s_all - s_low) * (1.0 / DIL_HEAD_DIM) + EPS)
        return jnp.where(low, r_low, r_high)

    for p in range(N_PAIRS):
        q = u[:, C_DQ + p * LANE:C_DQ + (p + 1) * LANE]
        qs = u[:, C_DQS + p * LANE:C_DQS + (p + 1) * LANE]
        qo = (q * cos_q + qs * sin_q) * (pair_scale(q) * b_scale)
        qb_ref[:, 2 * p * LANE:(2 * p + 1) * LANE] = jnp.where(low, qo, 0.0).astype(BF16)
        qb_ref[:, (2 * p + 1) * LANE:(2 * p + 2) * LANE] = jnp.where(low, 0.0, qo).astype(BF16)
        k = u[:, C_DK + p * LANE:C_DK + (p + 1) * LANE]
        ks = u[:, C_DKS + p * LANE:C_DKS + (p + 1) * LANE]
        kb_ref[:, p * LANE:(p + 1) * LANE] = ((k * cos_k + ks * sin_k) * pair_scale(k)).astype(BF16)
    vb_ref[...] = u[:, C_DV:C_DV + WIDTH_B].astype(BF16)
    uc_ref[...] = u[:, C_UC:C_UC + SSM_CH].astype(BF16)


def _inproj(x, sc1, sh1, n1, win, qg, wuq, kvg, wukv, a_gains, b_gains, tabs):
    bsz, seq, d = x.shape
    tm = TOKEN_TILE
    tok = lambda w: pl.BlockSpec((None, tm, w), lambda b, t: (b, t, 0))
    per_b = pl.BlockSpec((None, 1, d), lambda b, t: (b, 0, 0))
    outs = [(QA_W, BF16), (QA_W, BF16), (WIDTH_A, BF16), (2 * WIDTH_B, BF16), (WIDTH_B, BF16),
            (WIDTH_B, BF16), (SSM_CH, BF16)]
    return pl.pallas_call(
        _inproj_kernel,
        grid=(bsz, seq // tm),
        in_specs=[tok(d), per_b, per_b, _resident((1, d)), _resident(win.shape), _resident(qg.shape),
                  _resident(wuq.shape), _resident(kvg.shape), _resident(wukv.shape),
                  _resident(a_gains.shape), _resident(b_gains.shape),
                  tok(LANE), tok(LANE), tok(LANE), tok(LANE)],
        out_specs=[tok(w) for w, _ in outs],
        out_shape=[jax.ShapeDtypeStruct((bsz, seq, w), dt) for w, dt in outs],
        compiler_params=_cparams("parallel", "parallel"),
        name="inproj",
    )(x, sc1, sh1, n1, win, qg, wuq, kvg, wukv, a_gains, b_gains, *tabs)


def _attn_kernel(*refs, k_slabs, has_bias):
    if has_bias:
        q_ref, k_ref, v_ref, bias_ref, o_ref = refs
    else:
        q_ref, k_ref, v_ref, o_ref = refs
    v = v_ref[...]
    outs = []
    for j in range(2):
        q = q_ref[:, j * LANE:(j + 1) * LANE]
        k = k_ref[:, j * LANE:(j + 1) * LANE] if k_slabs == 2 else k_ref[...]
        s = lax.dot_general(q, k, (((1,), (1,)), ((), ())), preferred_element_type=F32)
        if has_bias:
            s = s + bias_ref[...]
        m = jnp.max(s, axis=-1, keepdims=True)
        p = jnp.exp(s - m)
        den = jnp.sum(p, axis=-1, keepdims=True)
        o = jnp.dot(p.astype(BF16), v, preferred_element_type=F32)
        outs.append(o * (1.0 / den))
    low = lax.broadcasted_iota(jnp.int32, outs[0].shape, 1) < HALF
    o_ref[...] = jnp.where(low, outs[0], outs[1])


def _attention(q, k, v, bias, name):
    bsz, seq, _ = q.shape
    tq = ATTN_Q_TILE
    k_slabs = k.shape[-1] // (N_PAIRS * LANE)
    in_specs = [
        pl.BlockSpec((None, tq, 2 * LANE), lambda i, b, p: (b, i, p)),
        pl.BlockSpec((None, seq, k_slabs * LANE), lambda i, b, p: (b, 0, p)),
        pl.BlockSpec((None, seq, LANE), lambda i, b, p: (b, 0, p)),
    ]
    args = [q, k, v]
    if bias is not None:
        in_specs.append(pl.BlockSpec((tq, seq), lambda i, b, p: (i, 0)))
        args.append(bias)
    return pl.pallas_call(
        functools.partial(_attn_kernel, k_slabs=k_slabs, has_bias=bias is not None),
        grid=(seq // tq, bsz, N_PAIRS),
        in_specs=in_specs,
        out_specs=pl.BlockSpec((None, tq, LANE), lambda i, b, p: (b, i, p)),
        out_shape=jax.ShapeDtypeStruct((bsz, seq, N_PAIRS * LANE), F32),
        compiler_params=_cparams("parallel", "parallel", "parallel"),
        name=name,
    )(*args)


def _dilated_bias(seq):
    idx = jnp.arange(seq, dtype=jnp.int32)
    delta = jnp.abs(idx[None, :] - idx[:, None])
    count = jnp.zeros((seq, seq), F32)
    for window, dil in DIL_PATTERNS:
        half = window // (2 * dil)
        count += ((delta % dil == 0) & (delta <= half * dil)).astype(F32)
    return jnp.where(count > 0, jnp.log(jnp.maximum(count, 1.0)), NEG_INF)


def _s5_kernel(u_ref, wz_ref, m_ref, v_ref, at_ref, y_ref, z_scr, s_scr, *, n_chunks, bsz):
    u = u_ref[...]
    z_scr[...] = jnp.dot(u, wz_ref[...], preferred_element_type=F32)
    at = at_ref[...]
    ar_f, ai_f, ar_b, ai_b = at[0:1, :], at[1:2, :], at[2:3, :], at[3:4, :]

    def fwd(kk, carry):
        re, im = carry
        r0 = pl.multiple_of(kk * bsz, bsz)
        s_scr[pl.ds(r0, bsz), 0:LANE] = re
        s_scr[pl.ds(r0, bsz), LANE:2 * LANE] = im
        zr = z_scr[pl.ds(r0, bsz), 0:LANE]
        zi = z_scr[pl.ds(r0, bsz), LANE:2 * LANE]
        return ar_f * re - ai_f * im + zr, ar_f * im + ai_f * re + zi

    def bwd(i, carry):
        re, im = carry
        r0 = pl.multiple_of((n_chunks - 1 - i) * bsz, bsz)
        s_scr[pl.ds(r0, bsz), 2 * LANE:3 * LANE] = re
        s_scr[pl.ds(r0, bsz), 3 * LANE:4 * LANE] = im
        zr = z_scr[pl.ds(r0, bsz), 2 * LANE:3 * LANE]
        zi = z_scr[pl.ds(r0, bsz), 3 * LANE:4 * LANE]
        return ar_b * re - ai_b * im + zr, ar_b * im + ai_b * re + zi

    zero = jnp.zeros((bsz, LANE), F32)
    lax.fori_loop(0, n_chunks, fwd, (zero, zero))
    lax.fori_loop(0, n_chunks, bwd, (zero, zero))
    y = jnp.dot(u, m_ref[...], preferred_element_type=F32)
    y += jnp.dot(s_scr[...].astype(BF16), v_ref[...], preferred_element_type=F32)
    y_ref[...] = y


def _s5(u_g, wz, mm, vv, at, bsz):
    groups, rows, width = u_g.shape
    n_chunks = rows // bsz
    sw = 4 * LANE
    return pl.pallas_call(
        functools.partial(_s5_kernel, n_chunks=n_chunks, bsz=bsz),
        grid=(groups,),
        in_specs=[
            pl.BlockSpec((None, rows, width), lambda g: (g, 0, 0)),
            pl.BlockSpec((None, width, sw), lambda g: (g, 0, 0)),
            pl.BlockSpec((None, width, width), lambda g: (g, 0, 0)),
            pl.BlockSpec((None, sw, width), lambda g: (g, 0, 0)),
            pl.BlockSpec((None, SUBLANE, LANE), lambda g: (g, 0, 0)),
        ],
        out_specs=pl.BlockSpec((None, rows, width), lambda g: (g, 0, 0)),
        out_shape=jax.ShapeDtypeStruct((groups, rows, width), F32),
        scratch_shapes=[pltpu.VMEM((rows, sw), F32), pltpu.VMEM((rows, sw), F32)],
        compiler_params=_cparams("parallel"),
        name="s5",
    )(u_g, wz, mm, vv, at)


def _s5_matrices(a_re, a_im, log_dt, b_re, b_im, c_re, c_im, d_skip):
    t = SSM_CHUNK
    g, p, cg = SSM_GROUPS, SSM_STATE, SSM_GROUP_CH
    a = lax.complex(a_re.astype(F32), a_im.astype(F32))
    dt = jnp.exp(log_dt.astype(F32))[..., None]
    lam = a * dt
    a_bar = jnp.exp(lam)
    b_bar = ((a_bar - 1.0) / a)[..., None] * lax.complex(b_re.astype(F32), b_im.astype(F32))
    c = lax.complex(c_re.astype(F32), c_im.astype(F32))
    n = jnp.arange(t + 1, dtype=F32)
    pw = jnp.exp(lam[:, None] * n[None, :, None, None].astype(jnp.complex64))
    kern = jnp.einsum("dgcp,dtgp,dgpe->dtgce", c, pw[:, :t], b_bar).real
    lag = jnp.arange(t)[None, :] - jnp.arange(t)[:, None]
    kf = kern[0][jnp.clip(lag, 0, t - 1)]
    kb = kern[1][jnp.clip(-lag, 0, t - 1)]
    sel = lag[:, :, None, None, None]
    eye = jnp.eye(cg, dtype=F32) * d_skip.astype(F32).reshape(g, cg)[:, :, None]
    full = jnp.where(sel > 0, kf, jnp.where(sel < 0, kb, kf + kb + eye[None, None]))
    mm = full.transpose(2, 0, 4, 1, 3).reshape(g, t * cg, t * cg)
    wf = pw[0, :t][::-1][:, :, :, None] * b_bar[0][None]
    wb = pw[1, :t][:, :, :, None] * b_bar[1][None]
    zpad = jnp.zeros((g, t * cg, LANE - p), F32)

    def in_cols(w):
        w = w.transpose(1, 0, 3, 2).reshape(g, t * cg, p)
        return [w.real, zpad, w.imag, zpad]

    wz = jnp.concatenate(in_cols(wf) + in_cols(wb), axis=-1)
    vf = c[0][:, None] * pw[0, 1:t + 1].transpose(1, 0, 2)[:, :, None, :]
    vb = c[1][:, None] * pw[1, 1:t + 1][::-1].transpose(1, 0, 2)[:, :, None, :]
    vpad = jnp.zeros((g, LANE - p, t * cg), F32)

    def out_rows(v):
        v = v.transpose(0, 3, 1, 2).reshape(g, p, t * cg)
        return [v.real, vpad, -v.imag, vpad]

    vv = jnp.concatenate(out_rows(vf) + out_rows(vb), axis=1)
    at_c = pw[:, t]
    lane_pad = jnp.zeros((g, LANE - p), F32)
    rows = [jnp.concatenate([at_c[0].real, lane_pad], -1), jnp.concatenate([at_c[0].imag, lane_pad], -1),
            jnp.concatenate([at_c[1].real, lane_pad], -1), jnp.concatenate([at_c[1].imag, lane_pad], -1)]
    at = jnp.stack(rows + [jnp.zeros((g, LANE), F32)] * (SUBLANE - 4), axis=1)
    return wz.astype(BF16), mm.astype(BF16), vv.astype(BF16), at


def _mix_kernel(oa_ref, ob_ref, y_ref, x_ref, g1_ref, wglu_ref, bglu_ref, gm_ref, wo_ref, o_ref):
    y = y_ref[...]
    y = 0.5 * y * (1.0 + jnp.tanh(math.sqrt(2.0 / math.pi) * (y + 0.044715 * (y * y * y))))
    z = _bdot(y, wglu_ref[...]) + bglu_ref[...]
    gate = z[:, SSM_CH:]
    oc = z[:, :SSM_CH] * (1.0 / (1.0 + jnp.exp(-gate)))
    oa, ob = oa_ref[...], ob_ref[...]
    na = oa * _rms_scale(oa, WIDTH_A) * gm_ref[:, 0:WIDTH_A]
    nb = ob * _rms_scale(ob, WIDTH_B) * gm_ref[:, WIDTH_A:WIDTH_A + WIDTH_B]
    nc = oc * _rms_scale(oc, SSM_CH) * gm_ref[:, WIDTH_A + WIDTH_B:]
    acc = _bdot(na, wo_ref[0:WIDTH_A, :])
    acc += _bdot(nb, wo_ref[WIDTH_A:WIDTH_A + WIDTH_B, :])
    acc += _bdot(nc, wo_ref[WIDTH_A + WIDTH_B:, :])
    o_ref[...] = x_ref[...] + g1_ref[...] * acc


def _mix(oa, ob, y, x, g1, wglu, bglu, gm, wo):
    bsz, seq, d = x.shape
    tm = TOKEN_TILE
    tok = lambda w: pl.BlockSpec((None, tm, w), lambda b, t: (b, t, 0))
    per_b = pl.BlockSpec((None, 1, d), lambda b, t: (b, 0, 0))
    return pl.pallas_call(
        _mix_kernel,
        grid=(bsz, seq // tm),
        in_specs=[tok(WIDTH_A), tok(WIDTH_B), tok(SSM_CH), tok(d), per_b, _resident(wglu.shape),
                  _resident(bglu.shape), _resident(gm.shape), _resident(wo.shape)],
        out_specs=tok(d),
        out_shape=jax.ShapeDtypeStruct((bsz, seq, d), F32),
        compiler_params=_cparams("parallel", "parallel"),
        name="mix",
    )(oa, ob, y, x, g1, wglu, bglu, gm, wo)


def _ffn_kernel(x_ref, xp_ref, xn_ref, sc_ref, sh_ref, g2_ref, n2_ref, wup_ref, cw_ref, cb_ref, wdn_ref,
                o_ref, h_scr, z_scr, *, n_tiles):
    tm = x_ref.shape[0]
    t = pl.program_id(1)
    mod_scale = n2_ref[...] * (1.0 + sc_ref[...])
    shift = sh_ref[...]

    def normed(v):
        return (v * _rms_scale(v, D_MODEL) * mod_scale + shift).astype(BF16)

    x = x_ref[...]
    not_first = (t > 0).astype(F32)
    not_last = (t < n_tiles - 1).astype(F32)
    h_scr[0:SUBLANE, :] = normed(xp_ref[...]).astype(F32) * not_first
    h_scr[SUBLANE:SUBLANE + tm, :] = normed(x).astype(F32)
    h_scr[SUBLANE + tm:, :] = normed(xn_ref[...]).astype(F32) * not_last
    h = h_scr[...].astype(BF16)
    fc = FFN_CHUNK
    acc = None
    for f in range(FFN_HIDDEN // fc):
        z_scr[...] = jnp.dot(h, wup_ref[:, 2 * f * fc:2 * (f + 1) * fc], preferred_element_type=F32)
        cw = cw_ref[:, 2 * f * fc:2 * (f + 1) * fc]
        prev = z_scr[SUBLANE - 1:SUBLANE - 1 + tm, :]
        cur = z_scr[SUBLANE:SUBLANE + tm, :]
        nxt = z_scr[SUBLANE + 1:SUBLANE + 1 + tm, :]
        conv = prev * cw[0:1, :] + cur * cw[1:2, :] + nxt * cw[2:3, :] + cb_ref[:, 2 * f * fc:2 * (f + 1) * fc]
        val, gate = conv[:, :fc], conv[:, fc:]
        act = (gate * (1.0 / (1.0 + jnp.exp(-gate))) * val).astype(BF16)
        part = jnp.dot(act, wdn_ref[f * fc:(f + 1) * fc, :], preferred_element_type=F32)
        acc = part if acc is None else acc + part
    o_ref[...] = x + g2_ref[...] * acc


def _ffn(x, sc2, sh2, g2, n2, wup, cw, cb, wdn):
    bsz, seq, d = x.shape
    tm = TOKEN_TILE
    n_tiles = seq // tm
    rows8 = tm // SUBLANE
    tok = pl.BlockSpec((None, tm, d), lambda b, t: (b, t, 0))
    prev8 = pl.BlockSpec((None, SUBLANE, d), lambda b, t: (b, jnp.maximum(t * rows8 - 1, 0), 0))
    next8 = pl.BlockSpec((None, SUBLANE, d),
                         lambda b, t: (b, jnp.minimum((t + 1) * rows8, seq // SUBLANE - 1), 0))
    per_b = pl.BlockSpec((None, 1, d), lambda b, t: (b, 0, 0))
    return pl.pallas_call(
        functools.partial(_ffn_kernel, n_tiles=n_tiles),
        grid=(bsz, n_tiles),
        in_specs=[tok, prev8, next8, per_b, per_b, per_b, _resident(n2.shape), _resident(wup.shape),
                  _resident(cw.shape), _resident(cb.shape), _resident(wdn.shape)],
        out_specs=tok,
        out_shape=jax.ShapeDtypeStruct((bsz, seq, d), F32),
        scratch_shapes=[pltpu.VMEM((tm + 2 * SUBLANE, d), F32),
                        pltpu.VMEM((tm + 2 * SUBLANE, 2 * FFN_CHUNK), F32)],
        compiler_params=_cparams("parallel", "parallel"),
        name="ffn",
    )(x, x, x, sc2, sh2, g2, n2, wup, cw, cb, wdn)


def _swap_halves(w, heads, dim):
    lead = w.shape[:-1]
    return jnp.flip(w.reshape(lead + (heads, 2, dim // 2)), axis=-2).reshape(lead + (heads * dim,))


def _prep_inproj(w_in, q_gain, w_uq, kv_gain, w_ukv, mla_gain, dil_gain):
    d = w_in.shape[0]
    z = lambda n: jnp.zeros((d, n), F32)
    a, b = w_in[:, :IN_A], w_in[:, IN_A:IN_A + IN_B]
    rope = a[:, MLA_Q_RANK + MLA_KV_RANK:]
    dq, dk, dv = b[:, :WIDTH_B], b[:, WIDTH_B:2 * WIDTH_B], b[:, 2 * WIDTH_B:]
    win = jnp.concatenate([
        a[:, :MLA_Q_RANK], z(Q_RANK_PAD - MLA_Q_RANK),
        a[:, MLA_Q_RANK:MLA_Q_RANK + MLA_KV_RANK],
        z(MLA_NOPE), rope, z(LANE - MLA_QK),
        z(MLA_NOPE), _swap_halves(rope, 1, MLA_ROPE), z(LANE - MLA_QK),
        dq, _swap_halves(dq, DIL_HEADS, DIL_HEAD_DIM),
        dk, _swap_halves(dk, DIL_HEADS, DIL_HEAD_DIM),
        dv, w_in[:, IN_A + IN_B:]], axis=1).astype(BF16)

    wq = w_uq.reshape(MLA_Q_RANK, MLA_HEADS, MLA_QK)
    zq = lambda n: jnp.zeros((MLA_Q_RANK, MLA_HEADS, n), F32)
    slab = jnp.concatenate([wq, zq(LANE - MLA_QK)], -1)
    slab_s = jnp.concatenate([zq(MLA_NOPE), jnp.flip(wq[..., MLA_NOPE:].reshape(
        MLA_Q_RANK, MLA_HEADS, 2, MLA_ROPE // 2), 2).reshape(MLA_Q_RANK, MLA_HEADS, MLA_ROPE),
        zq(LANE - MLA_QK)], -1)
    wuq = jnp.concatenate([slab.reshape(MLA_Q_RANK, QA_W), slab_s.reshape(MLA_Q_RANK, QA_W)], 1)
    wuq = jnp.pad(wuq, ((0, Q_RANK_PAD - MLA_Q_RANK), (0, 0))).astype(BF16)
    qg = jnp.pad(q_gain, (0, Q_RANK_PAD - MLA_Q_RANK)).reshape(1, Q_RANK_PAD)

    wkv = w_ukv.reshape(MLA_KV_RANK, MLA_HEADS, MLA_NOPE + MLA_V)
    k_slab = jnp.concatenate([wkv[..., :MLA_NOPE], jnp.zeros((MLA_KV_RANK, MLA_HEADS, LANE - MLA_NOPE), F32)], -1)
    wukv = jnp.concatenate([k_slab.reshape(MLA_KV_RANK, QA_W),
                            wkv[..., MLA_NOPE:].reshape(MLA_KV_RANK, WIDTH_A)], 1).astype(BF16)
    kvg = kv_gain.reshape(1, MLA_KV_RANK)

    def mla_rows(g):
        pad = jnp.zeros((LANE - MLA_QK,), F32)
        plain = jnp.concatenate([g, pad])
        swapped = jnp.concatenate([jnp.zeros((MLA_NOPE,), F32), g[MLA_NOPE + MLA_ROPE // 2:],
                                   g[MLA_NOPE:MLA_NOPE + MLA_ROPE // 2], pad])
        return [plain, swapped]

    def dil_rows(g):
        return [jnp.tile(g, 2), jnp.tile(_swap_halves(g, 1, DIL_HEAD_DIM), 2)]

    fill = [jnp.zeros((LANE,), F32)] * (SUBLANE - 4)
    a_gains = jnp.stack(mla_rows(mla_gain[0]) + mla_rows(mla_gain[1]) + fill)
    b_gains = jnp.stack(dil_rows(dil_gain[0]) + dil_rows(dil_gain[1]) + fill)
    return win, qg, wuq, kvg, wukv, a_gains, b_gains


def _rope_tables(positions):
    def tables(dim):
        inv_freq = 1.0 / (ROPE_THETA ** (jnp.arange(0, dim, 2, dtype=F32) / dim))
        ang = positions.astype(F32)[..., None] * inv_freq
        return jnp.cos(ang), jnp.sin(ang)

    cos_r, sin_r = tables(MLA_ROPE)
    cos_f, sin_f = tables(DIL_HEAD_DIM)
    lead = positions.shape
    ones = jnp.ones(lead + (MLA_NOPE,), F32)
    zeros = jnp.zeros(lead + (MLA_NOPE,), F32)
    pad = jnp.zeros(lead + (LANE - MLA_QK,), F32)
    ca = jnp.concatenate([ones, cos_r, cos_r, pad], -1)
    sa = jnp.concatenate([zeros, -sin_r, sin_r, pad], -1)
    cb = jnp.concatenate([cos_f, cos_f, cos_f, cos_f], -1)
    sb = jnp.concatenate([-sin_f, sin_f, -sin_f, sin_f], -1)
    return ca, sa, cb, sb


def _prep_ffn(w_up, conv_w, conv_b):
    def inter(w):
        lead = w.shape[:-1]
        w2 = w.reshape(lead + (2, FFN_HIDDEN // FFN_CHUNK, FFN_CHUNK))
        return jnp.swapaxes(w2, -3, -2).reshape(lead + (2 * FFN_HIDDEN,))

    return inter(w_up).astype(BF16), inter(conv_w), inter(conv_b).reshape(1, 2 * FFN_HIDDEN)


def kernel(x, c, positions, w_mod, b_mod, norm1, w_in, mla_q_norm, mla_w_uq, mla_kv_norm, mla_w_ukv, mla_qk_gain, dil_qk_gain, ssm_a_re, ssm_a_im, ssm_log_dt, ssm_b_re, ssm_b_im, ssm_c_re, ssm_c_im, ssm_d, ssm_w_glu, ssm_b_glu, mix_norm, w_out, norm2, ffn_w_up, ffn_conv_w, ffn_conv_b, ffn_w_down):
    bsz, seq, d = x.shape
    depth = w_mod.shape[0]
    assert d == D_MODEL and seq % TOKEN_TILE == 0 and seq % SSM_CHUNK == 0
    tabs = _rope_tables(positions)
    bias = _dilated_bias(seq)
    mod = _modulation(c, w_mod, b_mod)
    n_chunks = seq // SSM_CHUNK
    cg, t = SSM_GROUP_CH, SSM_CHUNK
    for l in range(depth):
        sh1, sc1, g1, sh2, sc2, g2 = (m.reshape(bsz, 1, d) for m in jnp.split(mod[l], N_MOD, axis=-1))
        win, qg, wuq, kvg, wukv, a_gains, b_gains = _prep_inproj(
            w_in[l], mla_q_norm[l], mla_w_uq[l], mla_kv_norm[l], mla_w_ukv[l], mla_qk_gain[l], dil_qk_gain[l])
        qa, ka, va, qb, kb, vb, uc = _inproj(x, sc1, sh1, norm1[l].reshape(1, d), win, qg, wuq, kvg, wukv,
                                             a_gains, b_gains, tabs)
        oa = _attention(qa, ka, va, None, "mla_attention")
        ob = _attention(qb, kb, vb, bias, "dilated_attention")
        wz, mm, vv, at = _s5_matrices(ssm_a_re[l], ssm_a_im[l], ssm_log_dt[l], ssm_b_re[l], ssm_b_im[l],
                                      ssm_c_re[l], ssm_c_im[l], ssm_d[l])
        u_g = uc.reshape(bsz, n_chunks, t, SSM_GROUPS, cg).transpose(3, 1, 0, 2, 4)
        u_g = u_g.reshape(SSM_GROUPS, n_chunks * bsz, t * cg)
        y_g = _s5(u_g, wz, mm, vv, at, bsz)
        y = y_g.reshape(SSM_GROUPS, n_chunks, bsz, t, cg).transpose(2, 1, 3, 0, 4).reshape(bsz, seq, SSM_CH)
        x = _mix(oa, ob, y, x, g1, ssm_w_glu[l].astype(BF16), ssm_b_glu[l].reshape(1, -1),
                 mix_norm[l].reshape(1, -1), w_out[l].astype(BF16))
        wup, cw, cb = _prep_ffn(ffn_w_up[l], ffn_conv_w[l], ffn_conv_b[l])
        x = _ffn(x, sc2, sh2, g2, norm2[l].reshape(1, d), wup, cw, cb, ffn_w_down[l].astype(BF16))
    return x
```

```python
import functools
import math

import jax
import jax.numpy as jnp
from jax import lax
from jax.experimental import pallas as pl
from jax.experimental.pallas import tpu as pltpu

F32 = jnp.float32
BF16 = jnp.bfloat16

D_MODEL = 1024
MLA_HEADS = 6
MLA_NOPE = 64
MLA_ROPE = 32
MLA_V = 64
MLA_QK = MLA_NOPE + MLA_ROPE
MLA_Q_RANK = 192
MLA_KV_RANK = 128
DIL_HEADS = 6
DIL_HEAD_DIM = 64
DIL_PATTERNS = ((128, 1), (512, 4), (2048, 16))
SSM_GROUPS = 16
SSM_GROUP_CH = 16
SSM_CH = SSM_GROUPS * SSM_GROUP_CH
SSM_STATE = 64
WIDTH_A = MLA_HEADS * MLA_V
WIDTH_B = DIL_HEADS * DIL_HEAD_DIM
MIX_WIDTH = WIDTH_A + WIDTH_B + SSM_CH
IN_A = MLA_Q_RANK + MLA_KV_RANK + MLA_ROPE
IN_B = 3 * WIDTH_B
FFN_HIDDEN = 2816
ROPE_THETA = 10000.0
EPS = 1e-6
NEG_INF = -1e30
N_MOD = 6

LANE = 128
SUBLANE = 8
HALF = LANE // 2
Q_RANK_PAD = 256
TOKEN_TILE = 512
ATTN_Q_TILE = 512
FFN_CHUNK = 256
SSM_CHUNK = 32
VMEM_LIMIT = 56 * 1024 * 1024

C_CQ = 0
C_CKV = C_CQ + Q_RANK_PAD
C_KR = C_CKV + MLA_KV_RANK
C_KRS = C_KR + LANE
C_DQ = C_KRS + LANE
C_DQS = C_DQ + WIDTH_B
C_DK = C_DQS + WIDTH_B
C_DKS = C_DK + WIDTH_B
C_DV = C_DKS + WIDTH_B
C_UC = C_DV + WIDTH_B
IN_EXT = C_UC + SSM_CH
N_PAIRS = MLA_HEADS // 2
QA_W = MLA_HEADS * LANE


def _cparams(*sem):
    return pltpu.CompilerParams(dimension_semantics=sem, vmem_limit_bytes=VMEM_LIMIT)


def _resident(shape):
    nd = len(shape)
    return pl.BlockSpec(shape, lambda *_: (0,) * nd, pipeline_mode=pl.Buffered(1))


def _rms_scale(v, width):
    return lax.rsqrt(jnp.sum(v * v, axis=-1, keepdims=True) * (1.0 / width) + EPS)


def _bdot(a, b):
    return jnp.dot(a.astype(BF16), b, preferred_element_type=F32)


def _mod_kernel(c_ref, w_ref, b_ref, o_ref):
    c = c_ref[...]
    ca = c * (1.0 / (1.0 + jnp.exp(-c)))
    w = w_ref[...]
    c_hi = ca.astype(BF16)
    c_lo = (ca - c_hi.astype(F32)).astype(BF16)
    w_hi = w.astype(BF16)
    w_lo = (w - w_hi.astype(F32)).astype(BF16)
    acc = jnp.dot(c_hi, w_hi, preferred_element_type=F32)
    acc += jnp.dot(c_lo, w_hi, preferred_element_type=F32)
    acc += jnp.dot(c_hi, w_lo, preferred_element_type=F32)
    o_ref[...] = acc + b_ref[...]


def _modulation(c, w_mod, b_mod):
    depth, d, n = w_mod.shape
    bsz = c.shape[0]
    tn = 1024
    return pl.pallas_call(
        _mod_kernel,
        grid=(depth, n // tn),
        in_specs=[
            pl.BlockSpec((bsz, d), lambda l, j: (0, 0)),
            pl.BlockSpec((None, d, tn), lambda l, j: (l, 0, j)),
            pl.BlockSpec((None, 1, tn), lambda l, j: (l, 0, j)),
        ],
        out_specs=pl.BlockSpec((None, bsz, tn), lambda l, j: (l, 0, j)),
        out_shape=jax.ShapeDtypeStruct((depth, bsz, n), F32),
        compiler_params=_cparams("parallel", "parallel"),
        name="modulation",
    )(c, w_mod, b_mod.reshape(depth, 1, n))


def _inproj_kernel(x_ref, sc_ref, sh_ref, n1_ref, win_ref, qg_ref, wuq_ref, kvg_ref, wukv_ref,
                   ag_ref, bg_ref, ca_ref, sa_ref, cb_ref, sb_ref,
                   qa_ref, ka_ref, va_ref, qb_ref, kb_ref, vb_ref, uc_ref):
    x = x_ref[...]
    h = x * _rms_scale(x, D_MODEL) * n1_ref[...]
    h = h * (1.0 + sc_ref[...]) + sh_ref[...]
    u = _bdot(h, win_ref[...])

    cq = u[:, C_CQ:C_CQ + Q_RANK_PAD]
    cqn = cq * _rms_scale(cq, MLA_Q_RANK) * qg_ref[...]
    q_all = _bdot(cqn, wuq_ref[...])
    ckv = u[:, C_CKV:C_CKV + MLA_KV_RANK]
    ckvn = ckv * _rms_scale(ckv, MLA_KV_RANK) * kvg_ref[...]
    kv_all = _bdot(ckvn, wukv_ref[...])
    kr = u[:, C_KR:C_KR + LANE]
    krs = u[:, C_KRS:C_KRS + LANE]

    ca, sa = ca_ref[...], sa_ref[...]
    gq, gqs, gk, gks = ag_ref[0:1, :], ag_ref[1:2, :], ag_ref[2:3, :], ag_ref[3:4, :]
    cos_q, sin_q = ca * gq, sa * gqs
    cos_k, sin_k = ca * gk, sa * gks
    krs_rot = krs * sin_k
    a_scale = MLA_QK ** -0.5
    for hd in range(MLA_HEADS):
        q = q_all[:, hd * LANE:(hd + 1) * LANE]
        qs = q_all[:, QA_W + hd * LANE:QA_W + (hd + 1) * LANE]
        r = _rms_scale(q, MLA_QK) * a_scale
        qa_ref[:, hd * LANE:(hd + 1) * LANE] = ((q * cos_q + qs * sin_q) * r).astype(BF16)
        k = kv_all[:, hd * LANE:(hd + 1) * LANE] + kr
        r = _rms_scale(k, MLA_QK)
        ka_ref[:, hd * LANE:(hd + 1) * LANE] = ((k * cos_k + krs_rot) * r).astype(BF16)
    va_ref[...] = kv_all[:, QA_W:QA_W + WIDTH_A].astype(BF16)

    cb, sb = cb_ref[...], sb_ref[...]
    gq, gqs, gk, gks = bg_ref[0:1, :], bg_ref[1:2, :], bg_ref[2:3, :], bg_ref[3:4, :]
    cos_q, sin_q = cb * gq, sb * gqs
    cos_k, sin_k = cb * gk, sb * gks
    low = lax.broadcasted_iota(jnp.int32, (x.shape[0], LANE), 1) < HALF
    b_scale = DIL_HEAD_DIM ** -0.5

    def pair_scale(v):
        sq = v * v
        s_all = jnp.sum(sq, axis=-1, keepdims=True)
        s_low = jnp.sum(jnp.where(low, sq, 0.0), axis=-1, keepdims=True)
        r_low = lax.rsqrt(s_low * (1.0 / DIL_HEAD_DIM) + EPS)
        r_high = lax.rsqrt((s_all - s_low) * (1.0 / DIL_HEAD_DIM) + EPS)
        return jnp.where(low, r_low, r_high)

    for p in range(N_PAIRS):
        q = u[:, C_DQ + p * LANE:C_DQ + (p + 1) * LANE]
        qs = u[:, C_DQS + p * LANE:C_DQS + (p + 1) * LANE]
        qo = (q * cos_q + qs * sin_q) * (pair_scale(q) * b_scale)
        qb_ref[:, 2 * p * LANE:(2 * p + 1) * LANE] = jnp.where(low, qo, 0.0).astype(BF16)
        qb_ref[:, (2 * p + 1) * LANE:(2 * p + 2) * LANE] = jnp.where(low, 0.0, qo).astype(BF16)
        k = u[:, C_DK + p * LANE:C_DK + (p + 1) * LANE]
        ks = u[:, C_DKS + p * LANE:C_DKS + (p + 1) * LANE]
        kb_ref[:, p * LANE:(p + 1) * LANE] = ((k * cos_k + ks * sin_k) * pair_scale(k)).astype(BF16)
    vb_ref[...] = u[:, C_DV:C_DV + WIDTH_B].astype(BF16)
    uc_ref[...] = u[:, C_UC:C_UC + SSM_CH]


def _inproj(x, sc1, sh1, n1, win, qg, wuq, kvg, wukv, a_gains, b_gains, tabs):
    bsz, seq, d = x.shape
    tm = TOKEN_TILE
    tok = lambda w: pl.BlockSpec((None, tm, w), lambda b, t: (b, t, 0))
    per_b = pl.BlockSpec((None, 1, d), lambda b, t: (b, 0, 0))
    outs = [(QA_W, BF16), (QA_W, BF16), (WIDTH_A, BF16), (2 * WIDTH_B, BF16), (WIDTH_B, BF16),
            (WIDTH_B, BF16), (SSM_CH, F32)]
    return pl.pallas_call(
        _inproj_kernel,
        grid=(bsz, seq // tm),
        in_specs=[tok(d), per_b, per_b, _resident((1, d)), _resident(win.shape), _resident(qg.shape),
                  _resident(wuq.shape), _resident(kvg.shape), _resident(wukv.shape),
                  _resident(a_gains.shape), _resident(b_gains.shape),
                  tok(LANE), tok(LANE), tok(LANE), tok(LANE)],
        out_specs=[tok(w) for w, _ in outs],
        out_shape=[jax.ShapeDtypeStruct((bsz, seq, w), dt) for w, dt in outs],
        compiler_params=_cparams("parallel", "parallel"),
        name="inproj",
    )(x, sc1, sh1, n1, win, qg, wuq, kvg, wukv, a_gains, b_gains, *tabs)


def _attn_kernel(*refs, k_slabs, has_bias):
    if has_bias:
        q_ref, k_ref, v_ref, bias_ref, o_ref = refs
    else:
        q_ref, k_ref, v_ref, o_ref = refs
    v = v_ref[...]
    outs = []
    for j in range(2):
        q = q_ref[:, j * LANE:(j + 1) * LANE]
        k = k_ref[:, j * LANE:(j + 1) * LANE] if k_slabs == 2 else k_ref[...]
        s = lax.dot_general(q, k, (((1,), (1,)), ((), ())), preferred_element_type=F32)
        if has_bias:
            s = s + bias_ref[...]
        m = jnp.max(s, axis=-1, keepdims=True)
        p = jnp.exp(s - m)
        den = jnp.sum(p, axis=-1, keepdims=True)
        o = jnp.dot(p.astype(BF16), v, preferred_element_type=F32)
        outs.append(o * (1.0 / den))
    low = lax.broadcasted_iota(jnp.int32, outs[0].shape, 1) < HALF
    o_ref[...] = jnp.where(low, outs[0], outs[1])


def _attention(q, k, v, bias, name):
    bsz, seq, _ = q.shape
    tq = ATTN_Q_TILE
    k_slabs = k.shape[-1] // (N_PAIRS * LANE)
    in_specs = [
        pl.BlockSpec((None, tq, 2 * LANE), lambda i, b, p: (b, i, p)),
        pl.BlockSpec((None, seq, k_slabs * LANE), lambda i, b, p: (b, 0, p)),
        pl.BlockSpec((None, seq, LANE), lambda i, b, p: (b, 0, p)),
    ]
    args = [q, k, v]
    if bias is not None:
        in_specs.append(pl.BlockSpec((tq, seq), lambda i, b, p: (i, 0)))
        args.append(bias)
    return pl.pallas_call(
        functools.partial(_attn_kernel, k_slabs=k_slabs, has_bias=bias is not None),
        grid=(seq // tq, bsz, N_PAIRS),
        in_specs=in_specs,
        out_specs=pl.BlockSpec((None, tq, LANE), lambda i, b, p: (b, i, p)),
        out_shape=jax.ShapeDtypeStruct((bsz, seq, N_PAIRS * LANE), F32),
        compiler_params=_cparams("parallel", "parallel", "parallel"),
        name=name,
    )(*args)


def _bias_kernel(o_ref):
    rows, seq = o_ref.shape
    row = lax.broadcasted_iota(jnp.int32, (rows, seq), 0) + pl.program_id(0) * rows
    col = lax.broadcasted_iota(jnp.int32, (rows, seq), 1)
    delta = jnp.abs(col - row)
    count = jnp.zeros((rows, seq), F32)
    for window, dil in DIL_PATTERNS:
        reach = (window // (2 * dil)) * dil
        hit = jnp.where(delta <= reach, 1.0, 0.0)
        count += jnp.where((delta & (dil - 1)) == 0, hit, 0.0)
    o_ref[...] = jnp.where(count > 0.0, jnp.log(jnp.maximum(count, 1.0)), NEG_INF)


def _dilated_bias(seq):
    assert all(dil & (dil - 1) == 0 for _, dil in DIL_PATTERNS)
    rows = ATTN_Q_TILE
    return pl.pallas_call(
        _bias_kernel,
        grid=(seq // rows,),
        out_specs=pl.BlockSpec((rows, seq), lambda i: (i, 0)),
        out_shape=jax.ShapeDtypeStruct((seq, seq), F32),
        compiler_params=_cparams("parallel"),
        name="dilated_bias",
    )()


def _s5_kernel(u_ref, wz_ref, m_ref, vt_ref, at_ref, y_ref, z_scr, s_scr, *, n_chunks, bsz):
    u = u_ref[...]
    z = jnp.dot(u, wz_ref[...], preferred_element_type=F32)
    for slab in range(4):
        z_scr[slab] = z[:, slab * LANE:(slab + 1) * LANE]
    at = at_ref[...]
    ar_f, ai_f, ar_b, ai_b = at[0:1, :], at[1:2, :], at[2:3, :], at[3:4, :]

    def chunk_rows(kk):
        return pl.ds(kk, bsz, stride=n_chunks)

    def fwd(kk, carry):
        re, im = carry
        s_scr[0, chunk_rows(kk), :] = re
        s_scr[1, chunk_rows(kk), :] = im
        zr = z_scr[0, chunk_rows(kk), :]
        zi = z_scr[1, chunk_rows(kk), :]
        return ar_f * re - ai_f * im + zr, ar_f * im + ai_f * re + zi

    def bwd(i, carry):
        re, im = carry
        kk = n_chunks - 1 - i
        s_scr[2, chunk_rows(kk), :] = re
        s_scr[3, chunk_rows(kk), :] = im
        zr = z_scr[2, chunk_rows(kk), :]
        zi = z_scr[3, chunk_rows(kk), :]
        return ar_b * re - ai_b * im + zr, ar_b * im + ai_b * re + zi

    zero = jnp.zeros((bsz, LANE), F32)
    lax.fori_loop(0, n_chunks, fwd, (zero, zero))
    lax.fori_loop(0, n_chunks, bwd, (zero, zero))
    y = jnp.dot(u, m_ref[...], preferred_element_type=F32)
    states = jnp.concatenate([s_scr[slab] for slab in range(4)], axis=1).astype(BF16)
    y += lax.dot_general(states, vt_ref[...], (((1,), (1,)), ((), ())), preferred_element_type=F32)
    y_ref[...] = y


def _s5(u_g, wz, mm, vt, at, bsz):
    groups, rows, width = u_g.shape
    n_chunks = rows // bsz
    sw = 4 * LANE
    return pl.pallas_call(
        functools.partial(_s5_kernel, n_chunks=n_chunks, bsz=bsz),
        grid=(groups,),
        in_specs=[
            pl.BlockSpec((None, rows, width), lambda g: (g, 0, 0)),
            pl.BlockSpec((None, width, sw), lambda g: (g, 0, 0)),
            pl.BlockSpec((None, width, width), lambda g: (g, 0, 0)),
            pl.BlockSpec((None, width, sw), lambda g: (g, 0, 0)),
            pl.BlockSpec((None, SUBLANE, LANE), lambda g: (g, 0, 0)),
        ],
        out_specs=pl.BlockSpec((None, rows, width), lambda g: (g, 0, 0)),
        out_shape=jax.ShapeDtypeStruct((groups, rows, width), F32),
        scratch_shapes=[pltpu.VMEM((4, rows, LANE), F32), pltpu.VMEM((4, rows, LANE), F32)],
        compiler_params=_cparams("parallel"),
        name="s5",
    )(u_g, wz, mm, vt, at)


PIECE = SSM_GROUP_CH
PIECES = LANE // PIECE
GROUP_TILES = SSM_CH // LANE


def _piece_gather(load_tile, shifts_and_tiles):
    acc = None
    for slot, (tile_idx, shift) in enumerate(shifts_and_tiles):
        src = load_tile(tile_idx)
        if shift % LANE:
            src = pltpu.roll(src, shift % LANE, axis=1)
        if acc is None:
            acc = src
        else:
            piece = lax.broadcasted_iota(jnp.int32, src.shape, 1) // PIECE
            acc = jnp.where(piece == slot, src, acc)
    return acc


def _s5_pack_kernel(z_ref, o_ref):
    for g in range(SSM_GROUPS):
        q = g % PIECES
        for jt in range(SSM_CHUNK // PIECES):
            plan = [((jt * PIECES + jj) * GROUP_TILES + g // PIECES, (jj - q) * PIECE) for jj in range(PIECES)]
            tile = _piece_gather(lambda i: z_ref[:, i * LANE:(i + 1) * LANE], plan)
            o_ref[g, :, jt * LANE:(jt + 1) * LANE] = tile.astype(BF16)


def _s5_unpack_kernel(y_ref, o_ref):
    for j in range(SSM_CHUNK):
        jj = j % PIECES
        for half in range(GROUP_TILES):
            plan = [((half * PIECES + q), (q - jj) * PIECE) for q in range(PIECES)]
            jt = j // PIECES
            tile = _piece_gather(lambda g: y_ref[g, :, jt * LANE:(jt + 1) * LANE], plan)
            o_ref[:, (j * GROUP_TILES + half) * LANE:(j * GROUP_TILES + half + 1) * LANE] = tile


def _s5_pack(uc):
    bsz, seq, ch = uc.shape
    n_chunks = seq // SSM_CHUNK
    z = uc.reshape(bsz, n_chunks, SSM_CHUNK * ch)
    width = SSM_CHUNK * SSM_GROUP_CH
    return pl.pallas_call(
        _s5_pack_kernel,
        grid=(bsz,),
        in_specs=[pl.BlockSpec((None, n_chunks, SSM_CHUNK * ch), lambda b: (b, 0, 0))],
        out_specs=pl.BlockSpec((SSM_GROUPS, n_chunks, width), lambda b: (0, b, 0)),
        out_shape=jax.ShapeDtypeStruct((SSM_GROUPS, bsz * n_chunks, width), BF16),
        compiler_params=_cparams("parallel"),
        name="s5_pack",
    )(z)


def _s5_unpack(y_g, bsz):
    groups, rows, width = y_g.shape
    n_chunks = rows // bsz
    out = pl.pallas_call(
        _s5_unpack_kernel,
        grid=(bsz,),
        in_specs=[pl.BlockSpec((groups, n_chunks, width), lambda b: (0, b, 0))],
        out_specs=pl.BlockSpec((None, n_chunks, SSM_CHUNK * SSM_CH), lambda b: (b, 0, 0)),
        out_shape=jax.ShapeDtypeStruct((bsz, n_chunks, SSM_CHUNK * SSM_CH), F32),
        compiler_params=_cparams("parallel"),
        name="s5_unpack",
    )(y_g)
    return out.reshape(bsz, n_chunks * SSM_CHUNK, SSM_CH)


def _s5_matrices(a_re, a_im, log_dt, b_re, b_im, c_re, c_im, d_skip):
    t = SSM_CHUNK
    g, p, cg = SSM_GROUPS, SSM_STATE, SSM_GROUP_CH
    a = lax.complex(a_re.astype(F32), a_im.astype(F32))
    dt = jnp.exp(log_dt.astype(F32))[..., None]
    lam = a * dt
    a_bar = jnp.exp(lam)
    b_bar = ((a_bar - 1.0) / a)[..., None] * lax.complex(b_re.astype(F32), b_im.astype(F32))
    c = lax.complex(c_re.astype(F32), c_im.astype(F32))
    n = jnp.arange(t + 1, dtype=F32)
    pw = jnp.exp(lam[:, None] * n[None, :, None, None].astype(jnp.complex64))
    kern = jnp.einsum("dgcp,dtgp,dgpe->dtgce", c, pw[:, :t], b_bar).real
    eye = jnp.eye(cg, dtype=F32) * d_skip.astype(F32).reshape(g, cg)[:, :, None]
    lags = jnp.concatenate([kern[1][1:][::-1], (kern[0][0] + kern[1][0] + eye)[None], kern[0][1:]], 0)
    strip = lags.transpose(1, 3, 0, 2).reshape(g, cg, (2 * t - 1) * cg)
    mm = jnp.stack([strip[:, :, (t - 1 - j) * cg:(2 * t - 1 - j) * cg] for j in range(t)], axis=1)
    mm = mm.reshape(g, t * cg, t * cg)
    pw_g = pw.transpose(0, 2, 1, 3)
    pad = jnp.zeros((g, t * cg, LANE - p), F32)

    def slabs(w, sign):
        w = w.reshape(g, t * cg, p)
        return [w.real, pad, sign * w.imag, pad]

    b_t = b_bar.transpose(0, 1, 3, 2)
    wf = pw_g[0, :, :t][:, ::-1][:, :, None, :] * b_t[0][:, None]
    wb = pw_g[1, :, :t][:, :, None, :] * b_t[1][:, None]
    wz = jnp.concatenate(slabs(wf, 1.0) + slabs(wb, 1.0), axis=-1)
    vf = pw_g[0, :, 1:t + 1][:, :, None, :] * c[0][:, None]
    vb = pw_g[1, :, 1:t + 1][:, ::-1][:, :, None, :] * c[1][:, None]
    vt = jnp.concatenate(slabs(vf, -1.0) + slabs(vb, -1.0), axis=-1)
    at_c = pw[:, t]
    lane_pad = jnp.zeros((g, LANE - p), F32)
    rows = [jnp.concatenate([at_c[0].real, lane_pad], -1), jnp.concatenate([at_c[0].imag, lane_pad], -1),
            jnp.concatenate([at_c[1].real, lane_pad], -1), jnp.concatenate([at_c[1].imag, lane_pad], -1)]
    at = jnp.stack(rows + [jnp.zeros((g, LANE), F32)] * (SUBLANE - 4), axis=1)
    return wz.astype(BF16), mm.astype(BF16), vt.astype(BF16), at


def _mix_kernel(oa_ref, ob_ref, y_ref, x_ref, g1_ref, wglu_ref, bglu_ref, gm_ref, wo_ref, o_ref):
    y = y_ref[...]
    y = 0.5 * y * (1.0 + jnp.tanh(math.sqrt(2.0 / math.pi) * (y + 0.044715 * (y * y * y))))
    z = _bdot(y, wglu_ref[...]) + bglu_ref[...]
    gate = z[:, SSM_CH:]
    oc = z[:, :SSM_CH] * (1.0 / (1.0 + jnp.exp(-gate)))
    oa, ob = oa_ref[...], ob_ref[...]
    na = oa * _rms_scale(oa, WIDTH_A) * gm_ref[:, 0:WIDTH_A]
    nb = ob * _rms_scale(ob, WIDTH_B) * gm_ref[:, WIDTH_A:WIDTH_A + WIDTH_B]
    nc = oc * _rms_scale(oc, SSM_CH) * gm_ref[:, WIDTH_A + WIDTH_B:]
    acc = _bdot(na, wo_ref[0:WIDTH_A, :])
    acc += _bdot(nb, wo_ref[WIDTH_A:WIDTH_A + WIDTH_B, :])
    acc += _bdot(nc, wo_ref[WIDTH_A + WIDTH_B:, :])
    o_ref[...] = x_ref[...] + g1_ref[...] * acc


def _mix(oa, ob, y, x, g1, wglu, bglu, gm, wo):
    bsz, seq, d = x.shape
    tm = TOKEN_TILE
    tok = lambda w: pl.BlockSpec((None, tm, w), lambda b, t: (b, t, 0))
    per_b = pl.BlockSpec((None, 1, d), lambda b, t: (b, 0, 0))
    return pl.pallas_call(
        _mix_kernel,
        grid=(bsz, seq // tm),
        in_specs=[tok(WIDTH_A), tok(WIDTH_B), tok(SSM_CH), tok(d), per_b, _resident(wglu.shape),
                  _resident(bglu.shape), _resident(gm.shape), _resident(wo.shape)],
        out_specs=tok(d),
        out_shape=jax.ShapeDtypeStruct((bsz, seq, d), F32),
        compiler_params=_cparams("parallel", "parallel"),
        name="mix",
    )(oa, ob, y, x, g1, wglu, bglu, gm, wo)


def _ffn_kernel(x_ref, xp_ref, xn_ref, sc_ref, sh_ref, g2_ref, n2_ref, wup_ref, cw_ref, cb_ref, wdn_ref,
                o_ref, h_scr, z_scr, *, n_tiles):
    tm = x_ref.shape[0]
    t = pl.program_id(1)
    mod_scale = n2_ref[...] * (1.0 + sc_ref[...])
    shift = sh_ref[...]

    def normed(v):
        return (v * _rms_scale(v, D_MODEL) * mod_scale + shift).astype(BF16)

    x = x_ref[...]
    not_first = (t > 0).astype(F32)
    not_last = (t < n_tiles - 1).astype(F32)
    h_scr[0:SUBLANE, :] = normed(xp_ref[...]).astype(F32) * not_first
    h_scr[SUBLANE:SUBLANE + tm, :] = normed(x).astype(F32)
    h_scr[SUBLANE + tm:, :] = normed(xn_ref[...]).astype(F32) * not_last
    h = h_scr[...].astype(BF16)
    fc = FFN_CHUNK
    acc = None
    def conv_cols(slot, col0):
        z_scr[slot] = jnp.dot(h, wup_ref[:, col0:col0 + fc], preferred_element_type=F32)
        cw = cw_ref[:, col0:col0 + fc]
        prev = z_scr[slot, SUBLANE - 1:SUBLANE - 1 + tm, :]
        cur = z_scr[slot, SUBLANE:SUBLANE + tm, :]
        nxt = z_scr[slot, SUBLANE + 1:SUBLANE + 1 + tm, :]
        return prev * cw[0:1, :] + cur * cw[1:2, :] + nxt * cw[2:3, :] + cb_ref[:, col0:col0 + fc]

    for f in range(FFN_HIDDEN // fc):
        val = conv_cols(0, f * fc)
        gate = conv_cols(1, FFN_HIDDEN + f * fc)
        act = (gate * (1.0 / (1.0 + jnp.exp(-gate))) * val).astype(BF16)
        part = jnp.dot(act, wdn_ref[f * fc:(f + 1) * fc, :], preferred_element_type=F32)
        acc = part if acc is None else acc + part
    o_ref[...] = x + g2_ref[...] * acc


def _ffn(x, sc2, sh2, g2, n2, wup, cw, cb, wdn):
    bsz, seq, d = x.shape
    tm = TOKEN_TILE
    n_tiles = seq // tm
    rows8 = tm // SUBLANE
    tok = pl.BlockSpec((None, tm, d), lambda b, t: (b, t, 0))
    prev8 = pl.BlockSpec((None, SUBLANE, d), lambda b, t: (b, jnp.maximum(t * rows8 - 1, 0), 0))
    next8 = pl.BlockSpec((None, SUBLANE, d),
                         lambda b, t: (b, jnp.minimum((t + 1) * rows8, seq // SUBLANE - 1), 0))
    per_b = pl.BlockSpec((None, 1, d), lambda b, t: (b, 0, 0))
    return pl.pallas_call(
        functools.partial(_ffn_kernel, n_tiles=n_tiles),
        grid=(bsz, n_tiles),
        in_specs=[tok, prev8, next8, per_b, per_b, per_b, _resident(n2.shape), _resident(wup.shape),
                  _resident(cw.shape), _resident(cb.shape), _resident(wdn.shape)],
        out_specs=tok,
        out_shape=jax.ShapeDtypeStruct((bsz, seq, d), F32),
        scratch_shapes=[pltpu.VMEM((tm + 2 * SUBLANE, d), F32),
                        pltpu.VMEM((2, tm + 2 * SUBLANE, FFN_CHUNK), F32)],
        compiler_params=_cparams("parallel", "parallel"),
        name="ffn",
    )(x, x, x, sc2, sh2, g2, n2, wup, cw, cb, wdn)


def _swap_halves(w, heads, dim):
    half = dim // 2
    parts = []
    for hd in range(heads):
        parts += [w[..., hd * dim + half:(hd + 1) * dim], w[..., hd * dim:hd * dim + half]]
    return jnp.concatenate(parts, axis=-1)


def _prep_inproj(w_in, q_gain, w_uq, kv_gain, w_ukv, mla_gain, dil_gain):
    d = w_in.shape[0]
    z = lambda n: jnp.zeros((d, n), F32)
    a, b = w_in[:, :IN_A], w_in[:, IN_A:IN_A + IN_B]
    rope = a[:, MLA_Q_RANK + MLA_KV_RANK:]
    dq, dk, dv = b[:, :WIDTH_B], b[:, WIDTH_B:2 * WIDTH_B], b[:, 2 * WIDTH_B:]
    win = jnp.concatenate([
        a[:, :MLA_Q_RANK], z(Q_RANK_PAD - MLA_Q_RANK),
        a[:, MLA_Q_RANK:MLA_Q_RANK + MLA_KV_RANK],
        z(MLA_NOPE), rope, z(LANE - MLA_QK),
        z(MLA_NOPE), _swap_halves(rope, 1, MLA_ROPE), z(LANE - MLA_QK),
        dq, _swap_halves(dq, DIL_HEADS, DIL_HEAD_DIM),
        dk, _swap_halves(dk, DIL_HEADS, DIL_HEAD_DIM),
        dv, w_in[:, IN_A + IN_B:]], axis=1).astype(BF16)

    wq = w_uq.reshape(MLA_Q_RANK, MLA_HEADS, MLA_QK)
    zq = lambda n: jnp.zeros((MLA_Q_RANK, MLA_HEADS, n), F32)
    slab = jnp.concatenate([wq, zq(LANE - MLA_QK)], -1)
    slab_s = jnp.concatenate([zq(MLA_NOPE), wq[..., MLA_NOPE + MLA_ROPE // 2:],
                              wq[..., MLA_NOPE:MLA_NOPE + MLA_ROPE // 2], zq(LANE - MLA_QK)], -1)
    wuq = jnp.concatenate([slab.reshape(MLA_Q_RANK, QA_W), slab_s.reshape(MLA_Q_RANK, QA_W)], 1)
    wuq = jnp.pad(wuq, ((0, Q_RANK_PAD - MLA_Q_RANK), (0, 0))).astype(BF16)
    qg = jnp.pad(q_gain, (0, Q_RANK_PAD - MLA_Q_RANK)).reshape(1, Q_RANK_PAD)

    wkv = w_ukv.reshape(MLA_KV_RANK, MLA_HEADS, MLA_NOPE + MLA_V)
    k_slab = jnp.concatenate([wkv[..., :MLA_NOPE], jnp.zeros((MLA_KV_RANK, MLA_HEADS, LANE - MLA_NOPE), F32)], -1)
    wukv = jnp.concatenate([k_slab.reshape(MLA_KV_RANK, QA_W),
                            wkv[..., MLA_NOPE:].reshape(MLA_KV_RANK, WIDTH_A)], 1).astype(BF16)
    kvg = kv_gain.reshape(1, MLA_KV_RANK)

    def mla_rows(g):
        pad = jnp.zeros((LANE - MLA_QK,), F32)
        plain = jnp.concatenate([g, pad])
        swapped = jnp.concatenate([jnp.zeros((MLA_NOPE,), F32), g[MLA_NOPE + MLA_ROPE // 2:],
                                   g[MLA_NOPE:MLA_NOPE + MLA_ROPE // 2], pad])
        return [plain, swapped]

    def dil_rows(g):
        return [jnp.tile(g, 2), jnp.tile(_swap_halves(g, 1, DIL_HEAD_DIM), 2)]

    fill = [jnp.zeros((LANE,), F32)] * (SUBLANE - 4)
    a_gains = jnp.stack(mla_rows(mla_gain[0]) + mla_rows(mla_gain[1]) + fill)
    b_gains = jnp.stack(dil_rows(dil_gain[0]) + dil_rows(dil_gain[1]) + fill)
    return win, qg, wuq, kvg, wukv, a_gains, b_gains


def _rope_tables(positions):
    def tables(dim):
        inv_freq = 1.0 / (ROPE_THETA ** (jnp.arange(0, dim, 2, dtype=F32) / dim))
        ang = positions.astype(F32)[..., None] * inv_freq
        return jnp.cos(ang), jnp.sin(ang)

    cos_r, sin_r = tables(MLA_ROPE)
    cos_f, sin_f = tables(DIL_HEAD_DIM)
    lead = positions.shape
    ones = jnp.ones(lead + (MLA_NOPE,), F32)
    zeros = jnp.zeros(lead + (MLA_NOPE,), F32)
    pad = jnp.zeros(lead + (LANE - MLA_QK,), F32)
    ca = jnp.concatenate([ones, cos_r, cos_r, pad], -1)
    sa = jnp.concatenate([zeros, -sin_r, sin_r, pad], -1)
    cb = jnp.concatenate([cos_f, cos_f, cos_f, cos_f], -1)
    sb = jnp.concatenate([-sin_f, sin_f, -sin_f, sin_f], -1)
    return ca, sa, cb, sb


def kernel(x, c, positions, w_mod, b_mod, norm1, w_in, mla_q_norm, mla_w_uq, mla_kv_norm, mla_w_ukv, mla_qk_gain, dil_qk_gain, ssm_a_re, ssm_a_im, ssm_log_dt, ssm_b_re, ssm_b_im, ssm_c_re, ssm_c_im, ssm_d, ssm_w_glu, ssm_b_glu, mix_norm, w_out, norm2, ffn_w_up, ffn_conv_w, ffn_conv_b, ffn_w_down):
    bsz, seq, d = x.shape
    depth = w_mod.shape[0]
    assert d == D_MODEL and seq % TOKEN_TILE == 0 and seq % SSM_CHUNK == 0
    tabs = _rope_tables(positions)
    bias = _dilated_bias(seq)
    mod = _modulation(c, w_mod, b_mod)
    for l in range(depth):
        sh1, sc1, g1, sh2, sc2, g2 = (m.reshape(bsz, 1, d) for m in jnp.split(mod[l], N_MOD, axis=-1))
        win, qg, wuq, kvg, wukv, a_gains, b_gains = _prep_inproj(
            w_in[l], mla_q_norm[l], mla_w_uq[l], mla_kv_norm[l], mla_w_ukv[l], mla_qk_gain[l], dil_qk_gain[l])
        qa, ka, va, qb, kb, vb, uc = _inproj(x, sc1, sh1, norm1[l].reshape(1, d), win, qg, wuq, kvg, wukv,
                                             a_gains, b_gains, tabs)
        oa = _attention(qa, ka, va, None, "mla_attention")
        ob = _attention(qb, kb, vb, bias, "dilated_attention")
        wz, mm, vt, at = _s5_matrices(ssm_a_re[l], ssm_a_im[l], ssm_log_dt[l], ssm_b_re[l], ssm_b_im[l],
                                      ssm_c_re[l], ssm_c_im[l], ssm_d[l])
        y = _s5_unpack(_s5(_s5_pack(uc), wz, mm, vt, at, bsz), bsz)
        x = _mix(oa, ob, y, x, g1, ssm_w_glu[l].astype(BF16), ssm_b_glu[l].reshape(1, -1),
                 mix_norm[l].reshape(1, -1), w_out[l].astype(BF16))
        x = _ffn(x, sc2, sh2, g2, norm2[l].reshape(1, d), ffn_w_up[l].astype(BF16), ffn_conv_w[l],
                 ffn_conv_b[l].reshape(1, -1), ffn_w_down[l].astype(BF16))
    return x
```

```python
import functools
import math

import jax
import jax.numpy as jnp
from jax import lax
from jax.experimental import pallas as pl
from jax.experimental.pallas import tpu as pltpu

F32 = jnp.float32
BF16 = jnp.bfloat16

D_MODEL = 1024
MLA_HEADS = 6
MLA_NOPE = 64
MLA_ROPE = 32
MLA_V = 64
MLA_QK = MLA_NOPE + MLA_ROPE
MLA_Q_RANK = 192
MLA_KV_RANK = 128
DIL_HEADS = 6
DIL_HEAD_DIM = 64
DIL_PATTERNS = ((128, 1), (512, 4), (2048, 16))
SSM_GROUPS = 16
SSM_GROUP_CH = 16
SSM_CH = SSM_GROUPS * SSM_GROUP_CH
SSM_STATE = 64
WIDTH_A = MLA_HEADS * MLA_V
WIDTH_B = DIL_HEADS * DIL_HEAD_DIM
MIX_WIDTH = WIDTH_A + WIDTH_B + SSM_CH
IN_A = MLA_Q_RANK + MLA_KV_RANK + MLA_ROPE
IN_B = 3 * WIDTH_B
FFN_HIDDEN = 2816
ROPE_THETA = 10000.0
EPS = 1e-6
NEG_INF = -1e30
N_MOD = 6

LANE = 128
SUBLANE = 8
HALF = LANE // 2
Q_RANK_PAD = 256
TOKEN_TILE = 512
FFN_TILE = 1024
ATTN_Q_TILE = 1024
DIL_Q_BLOCK = 128
FFN_CHUNK = 256
SSM_CHUNK = 32
VMEM_LIMIT = 56 * 1024 * 1024

C_CQ = 0
C_CKV = C_CQ + Q_RANK_PAD
C_KR = C_CKV + MLA_KV_RANK
C_KRS = C_KR + LANE
C_DQ = C_KRS + LANE
C_DQS = C_DQ + WIDTH_B
C_DK = C_DQS + WIDTH_B
C_DKS = C_DK + WIDTH_B
C_DV = C_DKS + WIDTH_B
C_UC = C_DV + WIDTH_B
IN_EXT = C_UC + SSM_CH
N_PAIRS = MLA_HEADS // 2
QA_W = MLA_HEADS * LANE


def _cparams(*sem):
    return pltpu.CompilerParams(dimension_semantics=sem, vmem_limit_bytes=VMEM_LIMIT)


def _resident(shape):
    nd = len(shape)
    return pl.BlockSpec(shape, lambda *_: (0,) * nd, pipeline_mode=pl.Buffered(1))


def _rms_scale(v, width):
    return lax.rsqrt(jnp.sum(v * v, axis=-1, keepdims=True) * (1.0 / width) + EPS)


def _bdot(a, b):
    return jnp.dot(a.astype(BF16), b, preferred_element_type=F32)


def _mod_kernel(c_ref, w_ref, b_ref, o_ref):
    c = c_ref[...]
    ca = c * (1.0 / (1.0 + jnp.exp(-c)))
    w = w_ref[...]
    c_hi = ca.astype(BF16)
    c_lo = (ca - c_hi.astype(F32)).astype(BF16)
    w_hi = w.astype(BF16)
    w_lo = (w - w_hi.astype(F32)).astype(BF16)
    acc = jnp.dot(c_hi, w_hi, preferred_element_type=F32)
    acc += jnp.dot(c_lo, w_hi, preferred_element_type=F32)
    acc += jnp.dot(c_hi, w_lo, preferred_element_type=F32)
    o_ref[...] = acc + b_ref[...]


def _modulation(c, w_mod, b_mod):
    depth, d, n = w_mod.shape
    bsz = c.shape[0]
    tn = 1024
    return pl.pallas_call(
        _mod_kernel,
        grid=(depth, n // tn),
        in_specs=[
            pl.BlockSpec((bsz, d), lambda l, j: (0, 0)),
            pl.BlockSpec((None, d, tn), lambda l, j: (l, 0, j)),
            pl.BlockSpec((None, 1, tn), lambda l, j: (l, 0, j)),
        ],
        out_specs=pl.BlockSpec((None, bsz, tn), lambda l, j: (l, 0, j)),
        out_shape=jax.ShapeDtypeStruct((depth, bsz, n), F32),
        compiler_params=_cparams("parallel", "parallel"),
        name="modulation",
    )(c, w_mod, b_mod.reshape(depth, 1, n))


def _inproj_kernel(x_ref, sc_ref, sh_ref, n1_ref, win_ref, qg_ref, wuq_ref, kvg_ref, wukv_ref,
                   ag_ref, bg_ref, ca_ref, sa_ref, cb_ref, sb_ref,
                   qa_ref, ka_ref, va_ref, qb_ref, kb_ref, vb_ref, uc_ref):
    x = x_ref[...]
    h = x * _rms_scale(x, D_MODEL) * n1_ref[...]
    h = h * (1.0 + sc_ref[...]) + sh_ref[...]
    u = _bdot(h, win_ref[...])

    cq = u[:, C_CQ:C_CQ + Q_RANK_PAD]
    cqn = cq * _rms_scale(cq, MLA_Q_RANK) * qg_ref[...]
    q_all = _bdot(cqn, wuq_ref[...])
    ckv = u[:, C_CKV:C_CKV + MLA_KV_RANK]
    ckvn = ckv * _rms_scale(ckv, MLA_KV_RANK) * kvg_ref[...]
    kv_all = _bdot(ckvn, wukv_ref[...])
    kr = u[:, C_KR:C_KR + LANE]
    krs = u[:, C_KRS:C_KRS + LANE]

    ca, sa = ca_ref[...], sa_ref[...]
    gq, gqs, gk, gks = ag_ref[0:1, :], ag_ref[1:2, :], ag_ref[2:3, :], ag_ref[3:4, :]
    cos_q, sin_q = ca * gq, sa * gqs
    cos_k, sin_k = ca * gk, sa * gks
    krs_rot = krs * sin_k
    a_scale = MLA_QK ** -0.5
    for hd in range(MLA_HEADS):
        q = q_all[:, hd * LANE:(hd + 1) * LANE]
        qs = q_all[:, QA_W + hd * LANE:QA_W + (hd + 1) * LANE]
        r = _rms_scale(q, MLA_QK) * a_scale
        qa_ref[:, hd * LANE:(hd + 1) * LANE] = ((q * cos_q + qs * sin_q) * r).astype(BF16)
        k = kv_all[:, hd * LANE:(hd + 1) * LANE] + kr
        r = _rms_scale(k, MLA_QK)
        ka_ref[:, hd * LANE:(hd + 1) * LANE] = ((k * cos_k + krs_rot) * r).astype(BF16)
    va_ref[...] = kv_all[:, QA_W:QA_W + WIDTH_A].astype(BF16)

    cb, sb = cb_ref[...], sb_ref[...]
    gq, gqs, gk, gks = bg_ref[0:1, :], bg_ref[1:2, :], bg_ref[2:3, :], bg_ref[3:4, :]
    cos_q, sin_q = cb * gq, sb * gqs
    cos_k, sin_k = cb * gk, sb * gks
    low = lax.broadcasted_iota(jnp.int32, (x.shape[0], LANE), 1) < HALF
    b_scale = DIL_HEAD_DIM ** -0.5

    def pair_scale(v):
        sq = v * v
        s_all = jnp.sum(sq, axis=-1, keepdims=True)
        s_low = jnp.sum(jnp.where(low, sq, 0.0), axis=-1, keepdims=True)
        r_low = lax.rsqrt(s_low * (1.0 / DIL_HEAD_DIM) + EPS)
        r_high = lax.rsqrt((s_all - s_low) * (1.0 / DIL_HEAD_DIM) + EPS)
        return jnp.where(low, r_low, r_high)

    for p in range(N_PAIRS):
        q = u[:, C_DQ + p * LANE:C_DQ + (p + 1) * LANE]
        qs = u[:, C_DQS + p * LANE:C_DQS + (p + 1) * LANE]
        qo = (q * cos_q + qs * sin_q) * (pair_scale(q) * b_scale)
        qb_ref[:, 2 * p * LANE:(2 * p + 1) * LANE] = jnp.where(low, qo, 0.0).astype(BF16)
        qb_ref[:, (2 * p + 1) * LANE:(2 * p + 2) * LANE] = jnp.where(low, 0.0, qo).astype(BF16)
        k = u[:, C_DK + p * LANE:C_DK + (p + 1) * LANE]
        ks = u[:, C_DKS + p * LANE:C_DKS + (p + 1) * LANE]
        kb_ref[:, p * LANE:(p + 1) * LANE] = ((k * cos_k + ks * sin_k) * pair_scale(k)).astype(BF16)
    vb_ref[...] = u[:, C_DV:C_DV + WIDTH_B].astype(BF16)
    uc_ref[...] = u[:, C_UC:C_UC + SSM_CH]


def _inproj(x, sc1, sh1, n1, win, qg, wuq, kvg, wukv, a_gains, b_gains, tabs):
    bsz, seq, d = x.shape
    tm = TOKEN_TILE
    tok = lambda w: pl.BlockSpec((None, tm, w), lambda b, t: (b, t, 0))
    per_b = pl.BlockSpec((None, 1, d), lambda b, t: (b, 0, 0))
    outs = [(QA_W, BF16), (QA_W, BF16), (WIDTH_A, BF16), (2 * WIDTH_B, BF16), (WIDTH_B, BF16),
            (WIDTH_B, BF16), (SSM_CH, F32)]
    return pl.pallas_call(
        _inproj_kernel,
        grid=(bsz, seq // tm),
        in_specs=[tok(d), per_b, per_b, _resident((1, d)), _resident(win.shape), _resident(qg.shape),
                  _resident(wuq.shape), _resident(kvg.shape), _resident(wukv.shape),
                  _resident(a_gains.shape), _resident(b_gains.shape),
                  tok(LANE), tok(LANE), tok(LANE), tok(LANE)],
        out_specs=[tok(w) for w, _ in outs],
        out_shape=[jax.ShapeDtypeStruct((bsz, seq, w), dt) for w, dt in outs],
        compiler_params=_cparams("parallel", "parallel"),
        name="inproj",
    )(x, sc1, sh1, n1, win, qg, wuq, kvg, wukv, a_gains, b_gains, *tabs)


def _attn_kernel(q_ref, k_ref, v_ref, o_ref):
    v = v_ref[...]
    outs = []
    for j in range(2):
        q = q_ref[:, j * LANE:(j + 1) * LANE]
        k = k_ref[:, j * LANE:(j + 1) * LANE]
        s = lax.dot_general(q, k, (((1,), (1,)), ((), ())), preferred_element_type=F32)
        m = jnp.max(s, axis=-1, keepdims=True)
        p = jnp.exp(s - m)
        den = jnp.sum(p, axis=-1, keepdims=True)
        o = jnp.dot(p.astype(BF16), v, preferred_element_type=F32)
        outs.append(o * (1.0 / den))
    low = lax.broadcasted_iota(jnp.int32, outs[0].shape, 1) < HALF
    o_ref[...] = jnp.where(low, outs[0], outs[1]).astype(o_ref.dtype)


def _attention(q, k, v):
    bsz, seq, _ = q.shape
    tq = ATTN_Q_TILE
    return pl.pallas_call(
        _attn_kernel,
        grid=(bsz, N_PAIRS, seq // tq),
        in_specs=[
            pl.BlockSpec((None, tq, 2 * LANE), lambda b, p, i: (b, i, p)),
            pl.BlockSpec((None, seq, 2 * LANE), lambda b, p, i: (b, 0, p)),
            pl.BlockSpec((None, seq, LANE), lambda b, p, i: (b, 0, p)),
        ],
        out_specs=pl.BlockSpec((None, tq, LANE), lambda b, p, i: (b, i, p)),
        out_shape=jax.ShapeDtypeStruct((bsz, seq, N_PAIRS * LANE), BF16),
        compiler_params=_cparams("parallel", "parallel", "parallel"),
        name="mla_attention",
    )(q, k, v)


def _dilated_kernel(q_ref, k_ref, v_ref, o_ref, qf, kf, vf, acc, mx):
    seq = k_ref.shape[0]
    kf[...] = k_ref[...].astype(F32)
    vf[...] = v_ref[...].astype(F32)
    for j in range(2):
        qf[j] = q_ref[:, j * LANE:(j + 1) * LANE].astype(F32)
    qb = DIL_Q_BLOCK
    for branch, (window, dil) in enumerate(DIL_PATTERNS):
        half = window // (2 * dil)
        n_sub = seq // dil
        kw = min(n_sub, qb + 2 * half)
        key_minus_query = (lax.broadcasted_iota(jnp.int32, (qb, kw), 1)
                           - lax.broadcasted_iota(jnp.int32, (qb, kw), 0))
        for res in range(dil):
            for j0 in range(0, n_sub, qb):
                k0 = min(max(j0 - half, 0), n_sub - kw)
                band = jnp.where(jnp.abs(key_minus_query + (k0 - j0)) <= half, 0.0, NEG_INF)
                q_rows = pl.ds(res + dil * j0, qb, stride=dil)
                k_rows = pl.ds(res + dil * k0, kw, stride=dil)
                kb = kf[k_rows, :].astype(BF16)
                v_win = vf[k_rows, :]
                low_k = lax.broadcasted_iota(jnp.int32, (kw, LANE), 1) < HALF
                for j in range(2):
                    vb = jnp.where(low_k == (j == 0), v_win, 1.0).astype(BF16)
                    q = qf[j, q_rows, :].astype(BF16)
                    s = lax.dot_general(q, kb, (((1,), (1,)), ((), ())), preferred_element_type=F32) + band
                    m = jnp.max(s, axis=-1, keepdims=True)
                    p = jnp.exp(s - m)
                    mx[branch, j, q_rows, :] = jnp.broadcast_to(m, (qb, LANE))
                    acc[branch, j, q_rows, :] = jnp.dot(p.astype(BF16), vb, preferred_element_type=F32)
    outs = []
    n_br = len(DIL_PATTERNS)
    for j in range(2):
        m_all = mx[0, j]
        for br in range(1, n_br):
            m_all = jnp.maximum(m_all, mx[br, j])
        tot = None
        for br in range(n_br):
            w = jnp.exp(mx[br, j] - m_all)
            tot = w * acc[br, j] if tot is None else tot + w * acc[br, j]
        outs.append(tot * (1.0 / pltpu.roll(tot, HALF, axis=1)))
    low = lax.broadcasted_iota(jnp.int32, (seq, LANE), 1) < HALF
    o_ref[...] = jnp.where(low, outs[0], outs[1]).astype(o_ref.dtype)


def _dilated_attention(q, k, v):
    bsz, seq, _ = q.shape
    assert all(seq % (dil * DIL_Q_BLOCK) == 0 for _, dil in DIL_PATTERNS)
    return pl.pallas_call(
        _dilated_kernel,
        grid=(bsz, N_PAIRS),
        in_specs=[
            pl.BlockSpec((None, seq, 2 * LANE), lambda b, p: (b, 0, p)),
            pl.BlockSpec((None, seq, LANE), lambda b, p: (b, 0, p)),
            pl.BlockSpec((None, seq, LANE), lambda b, p: (b, 0, p)),
        ],
        out_specs=pl.BlockSpec((None, seq, LANE), lambda b, p: (b, 0, p)),
        out_shape=jax.ShapeDtypeStruct((bsz, seq, N_PAIRS * LANE), BF16),
        scratch_shapes=[pltpu.VMEM((2, seq, LANE), F32), pltpu.VMEM((seq, LANE), F32),
                        pltpu.VMEM((seq, LANE), F32)]
        + [pltpu.VMEM((len(DIL_PATTERNS), 2, seq, LANE), F32)] * 2,
        compiler_params=_cparams("parallel", "parallel"),
        name="dilated_attention",
    )(q, k, v)


def _s5_kernel(u_ref, wz_ref, m_ref, vt_ref, at_ref, y_ref, z_scr, s_scr, *, n_chunks, bsz):
    u = u_ref[...]
    z = jnp.dot(u, wz_ref[...], preferred_element_type=F32)
    for slab in range(4):
        z_scr[slab] = z[:, slab * LANE:(slab + 1) * LANE]
    at = at_ref[...]
    ar_f, ai_f, ar_b, ai_b = at[0:1, :], at[1:2, :], at[2:3, :], at[3:4, :]

    def chunk_rows(kk):
        return pl.ds(kk, bsz, stride=n_chunks)

    def step(i, carry):
        re_f, im_f, re_b, im_b = carry
        kf, kb = chunk_rows(i), chunk_rows(n_chunks - 1 - i)
        s_scr[0, kf, :] = re_f
        s_scr[1, kf, :] = im_f
        s_scr[2, kb, :] = re_b
        s_scr[3, kb, :] = im_b
        return (ar_f * re_f - ai_f * im_f + z_scr[0, kf, :], ar_f * im_f + ai_f * re_f + z_scr[1, kf, :],
                ar_b * re_b - ai_b * im_b + z_scr[2, kb, :], ar_b * im_b + ai_b * re_b + z_scr[3, kb, :])

    zero = jnp.zeros((bsz, LANE), F32)
    lax.fori_loop(0, n_chunks, step, (zero, zero, zero, zero), unroll=2)
    y = jnp.dot(u, m_ref[...], preferred_element_type=F32)
    states = jnp.concatenate([s_scr[slab] for slab in range(4)], axis=1).astype(BF16)
    y += lax.dot_general(states, vt_ref[...], (((1,), (1,)), ((), ())), preferred_element_type=F32)
    y_ref[...] = y


def _s5(u_g, wz, mm, vt, at, bsz):
    groups, rows, width = u_g.shape
    n_chunks = rows // bsz
    sw = 4 * LANE
    return pl.pallas_call(
        functools.partial(_s5_kernel, n_chunks=n_chunks, bsz=bsz),
        grid=(groups,),
        in_specs=[
            pl.BlockSpec((None, rows, width), lambda g: (g, 0, 0)),
            pl.BlockSpec((None, width, sw), lambda g: (g, 0, 0)),
            pl.BlockSpec((None, width, width), lambda g: (g, 0, 0)),
            pl.BlockSpec((None, width, sw), lambda g: (g, 0, 0)),
            pl.BlockSpec((None, SUBLANE, LANE), lambda g: (g, 0, 0)),
        ],
        out_specs=pl.BlockSpec((None, rows, width), lambda g: (g, 0, 0)),
        out_shape=jax.ShapeDtypeStruct((groups, rows, width), F32),
        scratch_shapes=[pltpu.VMEM((4, rows, LANE), F32), pltpu.VMEM((4, rows, LANE), F32)],
        compiler_params=_cparams("parallel"),
        name="s5",
    )(u_g, wz, mm, vt, at)


PIECE = SSM_GROUP_CH
PIECES = LANE // PIECE
GROUP_TILES = SSM_CH // LANE


def _piece_gather(load_tile, shifts_and_tiles):
    acc = None
    for slot, (tile_idx, shift) in enumerate(shifts_and_tiles):
        src = load_tile(tile_idx)
        if shift % LANE:
            src = pltpu.roll(src, shift % LANE, axis=1)
        if acc is None:
            acc = src
        else:
            piece = lax.broadcasted_iota(jnp.int32, src.shape, 1) // PIECE
            acc = jnp.where(piece == slot, src, acc)
    return acc


def _s5_pack_kernel(z_ref, o_ref):
    for g in range(SSM_GROUPS):
        q = g % PIECES
        for jt in range(SSM_CHUNK // PIECES):
            plan = [((jt * PIECES + jj) * GROUP_TILES + g // PIECES, (jj - q) * PIECE) for jj in range(PIECES)]
            tile = _piece_gather(lambda i: z_ref[:, i * LANE:(i + 1) * LANE], plan)
            o_ref[g, :, jt * LANE:(jt + 1) * LANE] = tile.astype(BF16)


def _s5_unpack_kernel(y_ref, o_ref):
    for j in range(SSM_CHUNK):
        jj = j % PIECES
        for half in range(GROUP_TILES):
            plan = [((half * PIECES + q), (q - jj) * PIECE) for q in range(PIECES)]
            jt = j // PIECES
            tile = _piece_gather(lambda g: y_ref[g, :, jt * LANE:(jt + 1) * LANE], plan)
            o_ref[:, (j * GROUP_TILES + half) * LANE:(j * GROUP_TILES + half + 1) * LANE] = tile.astype(BF16)


def _s5_pack(uc):
    bsz, seq, ch = uc.shape
    n_chunks = seq // SSM_CHUNK
    z = uc.reshape(bsz, n_chunks, SSM_CHUNK * ch)
    width = SSM_CHUNK * SSM_GROUP_CH
    return pl.pallas_call(
        _s5_pack_kernel,
        grid=(bsz,),
        in_specs=[pl.BlockSpec((None, n_chunks, SSM_CHUNK * ch), lambda b: (b, 0, 0))],
        out_specs=pl.BlockSpec((SSM_GROUPS, n_chunks, width), lambda b: (0, b, 0)),
        out_shape=jax.ShapeDtypeStruct((SSM_GROUPS, bsz * n_chunks, width), BF16),
        compiler_params=_cparams("parallel"),
        name="s5_pack",
    )(z)


def _s5_unpack(y_g, bsz):
    groups, rows, width = y_g.shape
    n_chunks = rows // bsz
    out = pl.pallas_call(
        _s5_unpack_kernel,
        grid=(bsz,),
        in_specs=[pl.BlockSpec((groups, n_chunks, width), lambda b: (0, b, 0))],
        out_specs=pl.BlockSpec((None, n_chunks, SSM_CHUNK * SSM_CH), lambda b: (b, 0, 0)),
        out_shape=jax.ShapeDtypeStruct((bsz, n_chunks, SSM_CHUNK * SSM_CH), BF16),
        compiler_params=_cparams("parallel"),
        name="s5_unpack",
    )(y_g)
    return out.reshape(bsz, n_chunks * SSM_CHUNK, SSM_CH)


def _s5_matrices(a_re, a_im, log_dt, b_re, b_im, c_re, c_im, d_skip):
    t = SSM_CHUNK
    g, p, cg = SSM_GROUPS, SSM_STATE, SSM_GROUP_CH
    a = lax.complex(a_re.astype(F32), a_im.astype(F32))
    dt = jnp.exp(log_dt.astype(F32))[..., None]
    lam = a * dt
    a_bar = jnp.exp(lam)
    b_bar = ((a_bar - 1.0) / a)[..., None] * lax.complex(b_re.astype(F32), b_im.astype(F32))
    c = lax.complex(c_re.astype(F32), c_im.astype(F32))
    n = jnp.arange(t + 1, dtype=F32)
    pw = jnp.exp(lam[:, None] * n[None, :, None, None].astype(jnp.complex64))
    kern = jnp.einsum("dgcp,dtgp,dgpe->dtgce", c, pw[:, :t], b_bar).real
    eye = jnp.eye(cg, dtype=F32) * d_skip.astype(F32).reshape(g, cg)[:, :, None]
    lags = jnp.concatenate([kern[1][1:][::-1], (kern[0][0] + kern[1][0] + eye)[None], kern[0][1:]], 0)
    strip = lags.transpose(1, 3, 0, 2).reshape(g, cg, (2 * t - 1) * cg)
    mm = jnp.stack([strip[:, :, (t - 1 - j) * cg:(2 * t - 1 - j) * cg] for j in range(t)], axis=1)
    mm = mm.reshape(g, t * cg, t * cg)
    pw_g = pw.transpose(0, 2, 1, 3)
    pad = jnp.zeros((g, t * cg, LANE - p), F32)

    def slabs(w, sign):
        w = w.reshape(g, t * cg, p)
        return [w.real, pad, sign * w.imag, pad]

    b_t = b_bar.transpose(0, 1, 3, 2)
    wf = pw_g[0, :, :t][:, ::-1][:, :, None, :] * b_t[0][:, None]
    wb = pw_g[1, :, :t][:, :, None, :] * b_t[1][:, None]
    wz = jnp.concatenate(slabs(wf, 1.0) + slabs(wb, 1.0), axis=-1)
    vf = pw_g[0, :, 1:t + 1][:, :, None, :] * c[0][:, None]
    vb = pw_g[1, :, 1:t + 1][:, ::-1][:, :, None, :] * c[1][:, None]
    vt = jnp.concatenate(slabs(vf, -1.0) + slabs(vb, -1.0), axis=-1)
    at_c = pw[:, t]
    lane_pad = jnp.zeros((g, LANE - p), F32)
    rows = [jnp.concatenate([at_c[0].real, lane_pad], -1), jnp.concatenate([at_c[0].imag, lane_pad], -1),
            jnp.concatenate([at_c[1].real, lane_pad], -1), jnp.concatenate([at_c[1].imag, lane_pad], -1)]
    at = jnp.stack(rows + [jnp.zeros((g, LANE), F32)] * (SUBLANE - 4), axis=1)
    return wz.astype(BF16), mm.astype(BF16), vt.astype(BF16), at


def _mix_kernel(oa_ref, ob_ref, y_ref, x_ref, g1_ref, wglu_ref, bglu_ref, gm_ref, wo_ref, o_ref):
    y = y_ref[...].astype(F32)
    y = 0.5 * y * (1.0 + jnp.tanh(math.sqrt(2.0 / math.pi) * (y + 0.044715 * (y * y * y))))
    z = _bdot(y, wglu_ref[...]) + bglu_ref[...]
    gate = z[:, SSM_CH:]
    oc = z[:, :SSM_CH] * (1.0 / (1.0 + jnp.exp(-gate)))
    oa, ob = oa_ref[...].astype(F32), ob_ref[...].astype(F32)
    na = oa * _rms_scale(oa, WIDTH_A) * gm_ref[:, 0:WIDTH_A]
    nb = ob * _rms_scale(ob, WIDTH_B) * gm_ref[:, WIDTH_A:WIDTH_A + WIDTH_B]
    nc = oc * _rms_scale(oc, SSM_CH) * gm_ref[:, WIDTH_A + WIDTH_B:]
    acc = _bdot(na, wo_ref[0:WIDTH_A, :])
    acc += _bdot(nb, wo_ref[WIDTH_A:WIDTH_A + WIDTH_B, :])
    acc += _bdot(nc, wo_ref[WIDTH_A + WIDTH_B:, :])
    o_ref[...] = x_ref[...] + g1_ref[...] * acc


def _mix(oa, ob, y, x, g1, wglu, bglu, gm, wo):
    bsz, seq, d = x.shape
    tm = TOKEN_TILE
    tok = lambda w: pl.BlockSpec((None, tm, w), lambda b, t: (b, t, 0))
    per_b = pl.BlockSpec((None, 1, d), lambda b, t: (b, 0, 0))
    return pl.pallas_call(
        _mix_kernel,
        grid=(bsz, seq // tm),
        in_specs=[tok(WIDTH_A), tok(WIDTH_B), tok(SSM_CH), tok(d), per_b, _resident(wglu.shape),
                  _resident(bglu.shape), _resident(gm.shape), _resident(wo.shape)],
        out_specs=tok(d),
        out_shape=jax.ShapeDtypeStruct((bsz, seq, d), F32),
        compiler_params=_cparams("parallel", "parallel"),
        name="mix",
    )(oa, ob, y, x, g1, wglu, bglu, gm, wo)


def _ffn_kernel(x_ref, xp_ref, xn_ref, sc_ref, sh_ref, g2_ref, n2_ref, wup_ref, cw_ref, cb_ref, wdn_ref,
                o_ref, h_scr, z_scr, act_scr, *, n_tiles):
    tm = x_ref.shape[0]
    t = pl.program_id(1)
    mod_scale = n2_ref[...] * (1.0 + sc_ref[...])
    shift = sh_ref[...]

    def normed(v):
        return v * _rms_scale(v, D_MODEL) * mod_scale + shift

    x = x_ref[...]
    not_first = (t > 0).astype(F32)
    not_last = (t < n_tiles - 1).astype(F32)
    h_scr[0:SUBLANE, :] = normed(xp_ref[...]) * not_first
    h_scr[SUBLANE:SUBLANE + tm, :] = normed(x)
    h_scr[SUBLANE + tm:, :] = normed(xn_ref[...]) * not_last
    h = h_scr[...].astype(BF16)
    fc = FFN_CHUNK

    def conv_cols(slot, col0):
        z_scr[slot] = jnp.dot(h, wup_ref[:, col0:col0 + fc], preferred_element_type=F32)
        cw = cw_ref[:, col0:col0 + fc]
        prev = z_scr[slot, SUBLANE - 1:SUBLANE - 1 + tm, :]
        cur = z_scr[slot, SUBLANE:SUBLANE + tm, :]
        nxt = z_scr[slot, SUBLANE + 1:SUBLANE + 1 + tm, :]
        return prev * cw[0:1, :] + cur * cw[1:2, :] + nxt * cw[2:3, :] + cb_ref[:, col0:col0 + fc]

    for f in range(FFN_HIDDEN // fc):
        val = conv_cols(0, f * fc)
        gate = conv_cols(1, FFN_HIDDEN + f * fc)
        act_scr[:, f * fc:(f + 1) * fc] = (gate * (1.0 / (1.0 + jnp.exp(-gate))) * val).astype(BF16)
    down = jnp.dot(act_scr[...], wdn_ref[...], preferred_element_type=F32)
    o_ref[...] = x + g2_ref[...] * down


def _ffn(x, sc2, sh2, g2, n2, wup, cw, cb, wdn):
    bsz, seq, d = x.shape
    tm = FFN_TILE
    n_tiles = seq // tm
    rows8 = tm // SUBLANE
    tok = pl.BlockSpec((None, tm, d), lambda b, t: (b, t, 0))
    prev8 = pl.BlockSpec((None, SUBLANE, d), lambda b, t: (b, jnp.maximum(t * rows8 - 1, 0), 0))
    next8 = pl.BlockSpec((None, SUBLANE, d),
                         lambda b, t: (b, jnp.minimum((t + 1) * rows8, seq // SUBLANE - 1), 0))
    per_b = pl.BlockSpec((None, 1, d), lambda b, t: (b, 0, 0))
    return pl.pallas_call(
        functools.partial(_ffn_kernel, n_tiles=n_tiles),
        grid=(bsz, n_tiles),
        in_specs=[tok, prev8, next8, per_b, per_b, per_b, _resident(n2.shape), _resident(wup.shape),
                  _resident(cw.shape), _resident(cb.shape), _resident(wdn.shape)],
        out_specs=tok,
        out_shape=jax.ShapeDtypeStruct((bsz, seq, d), F32),
        scratch_shapes=[pltpu.VMEM((tm + 2 * SUBLANE, d), F32),
                        pltpu.VMEM((2, tm + 2 * SUBLANE, FFN_CHUNK), F32),
                        pltpu.VMEM((tm, FFN_HIDDEN), BF16)],
        compiler_params=_cparams("parallel", "parallel"),
        name="ffn",
    )(x, x, x, sc2, sh2, g2, n2, wup, cw, cb, wdn)


def _swap_halves(w, heads, dim):
    half = dim // 2
    parts = []
    for hd in range(heads):
        parts += [w[..., hd * dim + half:(hd + 1) * dim], w[..., hd * dim:hd * dim + half]]
    return jnp.concatenate(parts, axis=-1)


def _prep_inproj(w_in, q_gain, w_uq, kv_gain, w_ukv, mla_gain, dil_gain):
    d = w_in.shape[0]
    z = lambda n: jnp.zeros((d, n), F32)
    a, b = w_in[:, :IN_A], w_in[:, IN_A:IN_A + IN_B]
    rope = a[:, MLA_Q_RANK + MLA_KV_RANK:]
    dq, dk, dv = b[:, :WIDTH_B], b[:, WIDTH_B:2 * WIDTH_B], b[:, 2 * WIDTH_B:]
    win = jnp.concatenate([
        a[:, :MLA_Q_RANK], z(Q_RANK_PAD - MLA_Q_RANK),
        a[:, MLA_Q_RANK:MLA_Q_RANK + MLA_KV_RANK],
        z(MLA_NOPE), rope, z(LANE - MLA_QK),
        z(MLA_NOPE), _swap_halves(rope, 1, MLA_ROPE), z(LANE - MLA_QK),
        dq, _swap_halves(dq, DIL_HEADS, DIL_HEAD_DIM),
        dk, _swap_halves(dk, DIL_HEADS, DIL_HEAD_DIM),
        dv, w_in[:, IN_A + IN_B:]], axis=1).astype(BF16)

    wq = w_uq.reshape(MLA_Q_RANK, MLA_HEADS, MLA_QK)
    zq = lambda n: jnp.zeros((MLA_Q_RANK, MLA_HEADS, n), F32)
    slab = jnp.concatenate([wq, zq(LANE - MLA_QK)], -1)
    slab_s = jnp.concatenate([zq(MLA_NOPE), wq[..., MLA_NOPE + MLA_ROPE // 2:],
                              wq[..., MLA_NOPE:MLA_NOPE + MLA_ROPE // 2], zq(LANE - MLA_QK)], -1)
    wuq = jnp.concatenate([slab.reshape(MLA_Q_RANK, QA_W), slab_s.reshape(MLA_Q_RANK, QA_W)], 1)
    wuq = jnp.pad(wuq, ((0, Q_RANK_PAD - MLA_Q_RANK), (0, 0))).astype(BF16)
    qg = jnp.pad(q_gain, (0, Q_RANK_PAD - MLA_Q_RANK)).reshape(1, Q_RANK_PAD)

    wkv = w_ukv.reshape(MLA_KV_RANK, MLA_HEADS, MLA_NOPE + MLA_V)
    k_slab = jnp.concatenate([wkv[..., :MLA_NOPE], jnp.zeros((MLA_KV_RANK, MLA_HEADS, LANE - MLA_NOPE), F32)], -1)
    wukv = jnp.concatenate([k_slab.reshape(MLA_KV_RANK, QA_W),
                            wkv[..., MLA_NOPE:].reshape(MLA_KV_RANK, WIDTH_A)], 1).astype(BF16)
    kvg = kv_gain.reshape(1, MLA_KV_RANK)

    def mla_rows(g):
        pad = jnp.zeros((LANE - MLA_QK,), F32)
        plain = jnp.concatenate([g, pad])
        swapped = jnp.concatenate([jnp.zeros((MLA_NOPE,), F32), g[MLA_NOPE + MLA_ROPE // 2:],
                                   g[MLA_NOPE:MLA_NOPE + MLA_ROPE // 2], pad])
        return [plain, swapped]

    def dil_rows(g):
        return [jnp.tile(g, 2), jnp.tile(_swap_halves(g, 1, DIL_HEAD_DIM), 2)]

    fill = [jnp.zeros((LANE,), F32)] * (SUBLANE - 4)
    a_gains = jnp.stack(mla_rows(mla_gain[0]) + mla_rows(mla_gain[1]) + fill)
    b_gains = jnp.stack(dil_rows(dil_gain[0]) + dil_rows(dil_gain[1]) + fill)
    return win, qg, wuq, kvg, wukv, a_gains, b_gains


def _rope_tables(positions):
    def tables(dim):
        inv_freq = 1.0 / (ROPE_THETA ** (jnp.arange(0, dim, 2, dtype=F32) / dim))
        ang = positions.astype(F32)[..., None] * inv_freq
        return jnp.cos(ang), jnp.sin(ang)

    cos_r, sin_r = tables(MLA_ROPE)
    cos_f, sin_f = tables(DIL_HEAD_DIM)
    lead = positions.shape
    ones = jnp.ones(lead + (MLA_NOPE,), F32)
    zeros = jnp.zeros(lead + (MLA_NOPE,), F32)
    pad = jnp.zeros(lead + (LANE - MLA_QK,), F32)
    ca = jnp.concatenate([ones, cos_r, cos_r, pad], -1)
    sa = jnp.concatenate([zeros, -sin_r, sin_r, pad], -1)
    cb = jnp.concatenate([cos_f, cos_f, cos_f, cos_f], -1)
    sb = jnp.concatenate([-sin_f, sin_f, -sin_f, sin_f], -1)
    return ca, sa, cb, sb


def kernel(x, c, positions, w_mod, b_mod, norm1, w_in, mla_q_norm, mla_w_uq, mla_kv_norm, mla_w_ukv, mla_qk_gain, dil_qk_gain, ssm_a_re, ssm_a_im, ssm_log_dt, ssm_b_re, ssm_b_im, ssm_c_re, ssm_c_im, ssm_d, ssm_w_glu, ssm_b_glu, mix_norm, w_out, norm2, ffn_w_up, ffn_conv_w, ffn_conv_b, ffn_w_down):
    bsz, seq, d = x.shape
    depth = w_mod.shape[0]
    assert d == D_MODEL and seq % TOKEN_TILE == 0 and seq % SSM_CHUNK == 0
    tabs = _rope_tables(positions)
    mod = _modulation(c, w_mod, b_mod)
    for l in range(depth):
        sh1, sc1, g1, sh2, sc2, g2 = (m.reshape(bsz, 1, d) for m in jnp.split(mod[l], N_MOD, axis=-1))
        win, qg, wuq, kvg, wukv, a_gains, b_gains = _prep_inproj(
            w_in[l], mla_q_norm[l], mla_w_uq[l], mla_kv_norm[l], mla_w_ukv[l], mla_qk_gain[l], dil_qk_gain[l])
        qa, ka, va, qb, kb, vb, uc = _inproj(x, sc1, sh1, norm1[l].reshape(1, d), win, qg, wuq, kvg, wukv,
                                             a_gains, b_gains, tabs)
        oa = _attention(qa, ka, va)
        ob = _dilated_attention(qb, kb, vb)
        wz, mm, vt, at = _s5_matrices(ssm_a_re[l], ssm_a_im[l], ssm_log_dt[l], ssm_b_re[l], ssm_b_im[l],
                                      ssm_c_re[l], ssm_c_im[l], ssm_d[l])
        y = _s5_unpack(_s5(_s5_pack(uc), wz, mm, vt, at, bsz), bsz)
        x = _mix(oa, ob, y, x, g1, ssm_w_glu[l].astype(BF16), ssm_b_glu[l].reshape(1, -1),
                 mix_norm[l].reshape(1, -1), w_out[l].astype(BF16))
        x = _ffn(x, sc2, sh2, g2, norm2[l].reshape(1, d), ffn_w_up[l].astype(BF16), ffn_conv_w[l],
                 ffn_conv_b[l].reshape(1, -1), ffn_w_down[l].astype(BF16))
    return x
```

```python
import functools
import math

import jax
import jax.numpy as jnp
from jax import lax
from jax.experimental import pallas as pl
from jax.experimental.pallas import tpu as pltpu

F32 = jnp.float32
BF16 = jnp.bfloat16

D_MODEL = 1024
MLA_HEADS = 6
MLA_NOPE = 64
MLA_ROPE = 32
MLA_V = 64
MLA_QK = MLA_NOPE + MLA_ROPE
MLA_Q_RANK = 192
MLA_KV_RANK = 128
DIL_HEADS = 6
DIL_HEAD_DIM = 64
DIL_PATTERNS = ((128, 1), (512, 4), (2048, 16))
SSM_GROUPS = 16
SSM_GROUP_CH = 16
SSM_CH = SSM_GROUPS * SSM_GROUP_CH
SSM_STATE = 64
WIDTH_A = MLA_HEADS * MLA_V
WIDTH_B = DIL_HEADS * DIL_HEAD_DIM
MIX_WIDTH = WIDTH_A + WIDTH_B + SSM_CH
IN_A = MLA_Q_RANK + MLA_KV_RANK + MLA_ROPE
IN_B = 3 * WIDTH_B
FFN_HIDDEN = 2816
ROPE_THETA = 10000.0
EPS = 1e-6
NEG_INF = -1e30
N_MOD = 6

LANE = 128
SUBLANE = 8
HALF = LANE // 2
Q_RANK_PAD = 256
TOKEN_TILE = 512
FFN_TILE = 1024
ATTN_Q_TILE = 1024
ATTN_PV_SPLIT = 2
DIL_Q_BLOCK = 128
DIL_GROUP = 4
FFN_CHUNK = 256
SSM_CHUNK = 32
VMEM_LIMIT = 56 * 1024 * 1024

C_CQ = 0
C_CKV = C_CQ + Q_RANK_PAD
C_KR = C_CKV + MLA_KV_RANK
C_KRS = C_KR + LANE
C_DQ = C_KRS + LANE
C_DQS = C_DQ + WIDTH_B
C_DK = C_DQS + WIDTH_B
C_DKS = C_DK + WIDTH_B
C_DV = C_DKS + WIDTH_B
C_UC = C_DV + WIDTH_B
IN_EXT = C_UC + SSM_CH
N_PAIRS = MLA_HEADS // 2
QA_W = MLA_HEADS * LANE


def _cparams(*sem):
    return pltpu.CompilerParams(dimension_semantics=sem, vmem_limit_bytes=VMEM_LIMIT)


def _resident(shape):
    nd = len(shape)
    return pl.BlockSpec(shape, lambda *_: (0,) * nd, pipeline_mode=pl.Buffered(1))


def _rms_scale(v, width):
    return lax.rsqrt(jnp.sum(v * v, axis=-1, keepdims=True) * (1.0 / width) + EPS)


def _bdot(a, b):
    return jnp.dot(a.astype(BF16), b, preferred_element_type=F32)


def _mod_kernel(c_ref, w_ref, b_ref, o_ref):
    c = c_ref[...]
    ca = c * (1.0 / (1.0 + jnp.exp(-c)))
    w = w_ref[...]
    c_hi = ca.astype(BF16)
    c_lo = (ca - c_hi.astype(F32)).astype(BF16)
    w_hi = w.astype(BF16)
    w_lo = (w - w_hi.astype(F32)).astype(BF16)
    acc = jnp.dot(c_hi, w_hi, preferred_element_type=F32)
    acc += jnp.dot(c_lo, w_hi, preferred_element_type=F32)
    acc += jnp.dot(c_hi, w_lo, preferred_element_type=F32)
    o_ref[...] = acc + b_ref[...]


def _modulation(c, w_mod, b_mod):
    depth, d, n = w_mod.shape
    bsz = c.shape[0]
    tn = 1024
    return pl.pallas_call(
        _mod_kernel,
        grid=(depth, n // tn),
        in_specs=[
            pl.BlockSpec((bsz, d), lambda l, j: (0, 0)),
            pl.BlockSpec((None, d, tn), lambda l, j: (l, 0, j)),
            pl.BlockSpec((None, 1, tn), lambda l, j: (l, 0, j)),
        ],
        out_specs=pl.BlockSpec((None, bsz, tn), lambda l, j: (l, 0, j)),
        out_shape=jax.ShapeDtypeStruct((depth, bsz, n), F32),
        compiler_params=_cparams("parallel", "parallel"),
        name="modulation",
    )(c, w_mod, b_mod.reshape(depth, 1, n))


def _inproj_kernel(x_ref, sc_ref, sh_ref, n1_ref, win_ref, qg_ref, wuq_ref, kvg_ref, wukv_ref,
                   ag_ref, bg_ref, ca_ref, sa_ref, cb_ref, sb_ref,
                   qa_ref, ka_ref, va_ref, qb_ref, kb_ref, vb_ref, uc_ref):
    x = x_ref[...]
    h = x * _rms_scale(x, D_MODEL) * n1_ref[...]
    h = h * (1.0 + sc_ref[...]) + sh_ref[...]
    u = _bdot(h, win_ref[...])

    cq = u[:, C_CQ:C_CQ + Q_RANK_PAD]
    cqn = cq * _rms_scale(cq, MLA_Q_RANK) * qg_ref[...]
    q_all = _bdot(cqn, wuq_ref[...])
    ckv = u[:, C_CKV:C_CKV + MLA_KV_RANK]
    ckvn = ckv * _rms_scale(ckv, MLA_KV_RANK) * kvg_ref[...]
    kv_all = _bdot(ckvn, wukv_ref[...])
    kr = u[:, C_KR:C_KR + LANE]
    krs = u[:, C_KRS:C_KRS + LANE]

    ca, sa = ca_ref[...], sa_ref[...]
    gq, gqs, gk, gks = ag_ref[0:1, :], ag_ref[1:2, :], ag_ref[2:3, :], ag_ref[3:4, :]
    cos_q, sin_q = ca * gq, sa * gqs
    cos_k, sin_k = ca * gk, sa * gks
    krs_rot = krs * sin_k
    a_scale = MLA_QK ** -0.5
    for hd in range(MLA_HEADS):
        q = q_all[:, hd * LANE:(hd + 1) * LANE]
        qs = q_all[:, QA_W + hd * LANE:QA_W + (hd + 1) * LANE]
        r = _rms_scale(q, MLA_QK) * a_scale
        qa_ref[:, hd * LANE:(hd + 1) * LANE] = ((q * cos_q + qs * sin_q) * r).astype(BF16)
        k = kv_all[:, hd * LANE:(hd + 1) * LANE] + kr
        r = _rms_scale(k, MLA_QK)
        ka_ref[:, hd * LANE:(hd + 1) * LANE] = ((k * cos_k + krs_rot) * r).astype(BF16)
    va_ref[...] = kv_all[:, QA_W:QA_W + WIDTH_A].astype(BF16)

    cb, sb = cb_ref[...], sb_ref[...]
    gq, gqs, gk, gks = bg_ref[0:1, :], bg_ref[1:2, :], bg_ref[2:3, :], bg_ref[3:4, :]
    cos_q, sin_q = cb * gq, sb * gqs
    cos_k, sin_k = cb * gk, sb * gks
    low = lax.broadcasted_iota(jnp.int32, (x.shape[0], LANE), 1) < HALF
    b_scale = DIL_HEAD_DIM ** -0.5

    def pair_scale(v):
        sq = v * v
        s_all = jnp.sum(sq, axis=-1, keepdims=True)
        s_low = jnp.sum(jnp.where(low, sq, 0.0), axis=-1, keepdims=True)
        r_low = lax.rsqrt(s_low * (1.0 / DIL_HEAD_DIM) + EPS)
        r_high = lax.rsqrt((s_all - s_low) * (1.0 / DIL_HEAD_DIM) + EPS)
        return jnp.where(low, r_low, r_high)

    for p in range(N_PAIRS):
        q = u[:, C_DQ + p * LANE:C_DQ + (p + 1) * LANE]
        qs = u[:, C_DQS + p * LANE:C_DQS + (p + 1) * LANE]
        qo = (q * cos_q + qs * sin_q) * (pair_scale(q) * b_scale)
        qb_ref[:, 2 * p * LANE:(2 * p + 1) * LANE] = jnp.where(low, qo, 0.0).astype(BF16)
        qb_ref[:, (2 * p + 1) * LANE:(2 * p + 2) * LANE] = jnp.where(low, 0.0, qo).astype(BF16)
        k = u[:, C_DK + p * LANE:C_DK + (p + 1) * LANE]
        ks = u[:, C_DKS + p * LANE:C_DKS + (p + 1) * LANE]
        kb_ref[:, p * LANE:(p + 1) * LANE] = ((k * cos_k + ks * sin_k) * pair_scale(k)).astype(BF16)
    vb_ref[...] = u[:, C_DV:C_DV + WIDTH_B].astype(BF16)
    uc_ref[...] = u[:, C_UC:C_UC + SSM_CH]


def _inproj(x, sc1, sh1, n1, win, qg, wuq, kvg, wukv, a_gains, b_gains, tabs):
    bsz, seq, d = x.shape
    tm = TOKEN_TILE
    tok = lambda w: pl.BlockSpec((None, tm, w), lambda b, t: (b, t, 0))
    per_b = pl.BlockSpec((None, 1, d), lambda b, t: (b, 0, 0))
    outs = [(QA_W, BF16), (QA_W, BF16), (WIDTH_A, BF16), (2 * WIDTH_B, BF16), (WIDTH_B, BF16),
            (WIDTH_B, BF16), (SSM_CH, F32)]
    return pl.pallas_call(
        _inproj_kernel,
        grid=(bsz, seq // tm),
        in_specs=[tok(d), per_b, per_b, _resident((1, d)), _resident(win.shape), _resident(qg.shape),
                  _resident(wuq.shape), _resident(kvg.shape), _resident(wukv.shape),
                  _resident(a_gains.shape), _resident(b_gains.shape),
                  tok(LANE), tok(LANE), tok(LANE), tok(LANE)],
        out_specs=[tok(w) for w, _ in outs],
        out_shape=[jax.ShapeDtypeStruct((bsz, seq, w), dt) for w, dt in outs],
        compiler_params=_cparams("parallel", "parallel"),
        name="inproj",
    )(x, sc1, sh1, n1, win, qg, wuq, kvg, wukv, a_gains, b_gains, *tabs)


def _attn_kernel(q_ref, k_ref, v_ref, o_ref):
    v = v_ref[...]
    low_k = lax.broadcasted_iota(jnp.int32, v.shape, 1) < HALF
    outs = []
    for j in range(2):
        q = q_ref[:, j * LANE:(j + 1) * LANE]
        k = k_ref[:, j * LANE:(j + 1) * LANE]
        s = lax.dot_general(q, k, (((1,), (1,)), ((), ())), preferred_element_type=F32)
        m = jnp.max(s, axis=-1, keepdims=True)
        pb = jnp.exp(s - m).astype(BF16)
        vb = jnp.where(low_k == (j == 0), v, jnp.ones_like(v))
        rows = pb.shape[0] // ATTN_PV_SPLIT
        o = jnp.concatenate([jnp.dot(pb[i * rows:(i + 1) * rows], vb, preferred_element_type=F32)
                             for i in range(ATTN_PV_SPLIT)], axis=0)
        outs.append(o * (1.0 / pltpu.roll(o, HALF, axis=1)))
    low = lax.broadcasted_iota(jnp.int32, outs[0].shape, 1) < HALF
    o_ref[...] = jnp.where(low, outs[0], outs[1]).astype(o_ref.dtype)


def _attention(q, k, v):
    bsz, seq, _ = q.shape
    tq = ATTN_Q_TILE
    return pl.pallas_call(
        _attn_kernel,
        grid=(bsz, N_PAIRS, seq // tq),
        in_specs=[
            pl.BlockSpec((None, tq, 2 * LANE), lambda b, p, i: (b, i, p)),
            pl.BlockSpec((None, seq, 2 * LANE), lambda b, p, i: (b, 0, p)),
            pl.BlockSpec((None, seq, LANE), lambda b, p, i: (b, 0, p)),
        ],
        out_specs=pl.BlockSpec((None, tq, LANE), lambda b, p, i: (b, i, p)),
        out_shape=jax.ShapeDtypeStruct((bsz, seq, N_PAIRS * LANE), BF16),
        compiler_params=_cparams("parallel", "parallel", "parallel"),
        name="mla_attention",
    )(q, k, v)


def _dilated_kernel(q_ref, k_ref, v_ref, o_ref, qf, kf, vf, acc, mx):
    seq = k_ref.shape[0]
    kf[...] = k_ref[...].astype(F32)
    vf[...] = v_ref[...].astype(F32)
    for j in range(2):
        qf[j] = q_ref[:, j * LANE:(j + 1) * LANE].astype(F32)
    qb = DIL_Q_BLOCK
    for branch, (window, dil) in enumerate(DIL_PATTERNS):
        half = window // (2 * dil)
        n_sub = seq // dil
        kw = min(n_sub, qb + 2 * half)
        key_minus_query = (lax.broadcasted_iota(jnp.int32, (qb, kw), 1)
                           - lax.broadcasted_iota(jnp.int32, (qb, kw), 0))
        low_k = lax.broadcasted_iota(jnp.int32, (kw, LANE), 1) < HALF
        blocks = [(res, j0) for res in range(dil) for j0 in range(0, n_sub, qb)]
        for g0 in range(0, len(blocks), DIL_GROUP):
            chains = []
            for res, j0 in blocks[g0:g0 + DIL_GROUP]:
                k0 = min(max(j0 - half, 0), n_sub - kw)
                band = jnp.where(jnp.abs(key_minus_query + (k0 - j0)) <= half, 0.0, NEG_INF)
                q_rows = pl.ds(res + dil * j0, qb, stride=dil)
                k_rows = pl.ds(res + dil * k0, kw, stride=dil)
                kb = kf[k_rows, :].astype(BF16)
                v_win = vf[k_rows, :]
                for j in range(2):
                    vb = jnp.where(low_k == (j == 0), v_win, 1.0).astype(BF16)
                    q = qf[j, q_rows, :].astype(BF16)
                    s = lax.dot_general(q, kb, (((1,), (1,)), ((), ())), preferred_element_type=F32) + band
                    chains.append((j, q_rows, s, vb))
            probs = []
            for j, q_rows, s, vb in chains:
                m = jnp.max(s, axis=-1, keepdims=True)
                mx[branch, j, q_rows, :] = jnp.broadcast_to(m, (qb, LANE))
                probs.append(jnp.exp(s - m).astype(BF16))
            for (j, q_rows, s, vb), p in zip(chains, probs):
                acc[branch, j, q_rows, :] = jnp.dot(p, vb, preferred_element_type=F32)
    outs = []
    n_br = len(DIL_PATTERNS)
    for j in range(2):
        m_all = mx[0, j]
        for br in range(1, n_br):
            m_all = jnp.maximum(m_all, mx[br, j])
        tot = None
        for br in range(n_br):
            w = jnp.exp(mx[br, j] - m_all)
            tot = w * acc[br, j] if tot is None else tot + w * acc[br, j]
        outs.append(tot * (1.0 / pltpu.roll(tot, HALF, axis=1)))
    low = lax.broadcasted_iota(jnp.int32, (seq, LANE), 1) < HALF
    o_ref[...] = jnp.where(low, outs[0], outs[1]).astype(o_ref.dtype)


def _dilated_attention(q, k, v):
    bsz, seq, _ = q.shape
    assert all(seq % (dil * DIL_Q_BLOCK) == 0 for _, dil in DIL_PATTERNS)
    return pl.pallas_call(
        _dilated_kernel,
        grid=(bsz, N_PAIRS),
        in_specs=[
            pl.BlockSpec((None, seq, 2 * LANE), lambda b, p: (b, 0, p)),
            pl.BlockSpec((None, seq, LANE), lambda b, p: (b, 0, p)),
            pl.BlockSpec((None, seq, LANE), lambda b, p: (b, 0, p)),
        ],
        out_specs=pl.BlockSpec((None, seq, LANE), lambda b, p: (b, 0, p)),
        out_shape=jax.ShapeDtypeStruct((bsz, seq, N_PAIRS * LANE), BF16),
        scratch_shapes=[pltpu.VMEM((2, seq, LANE), F32), pltpu.VMEM((seq, LANE), F32),
                        pltpu.VMEM((seq, LANE), F32)]
        + [pltpu.VMEM((len(DIL_PATTERNS), 2, seq, LANE), F32)] * 2,
        compiler_params=_cparams("parallel", "parallel"),
        name="dilated_attention",
    )(q, k, v)


def _s5_kernel(u_ref, wz_ref, m_ref, vt_ref, at_ref, y_ref, z_scr, s_scr, *, n_chunks, bsz):
    u = u_ref[...]
    z = jnp.dot(u, wz_ref[...], preferred_element_type=F32)
    for slab in range(4):
        z_scr[slab] = z[:, slab * LANE:(slab + 1) * LANE]
    at = at_ref[...]
    ar_f, ai_f, ar_b, ai_b = at[0:1, :], at[1:2, :], at[2:3, :], at[3:4, :]

    def chunk_rows(kk):
        return pl.ds(kk, bsz, stride=n_chunks)

    def step(i, carry):
        re_f, im_f, re_b, im_b = carry
        kf, kb = chunk_rows(i), chunk_rows(n_chunks - 1 - i)
        s_scr[0, kf, :] = re_f
        s_scr[1, kf, :] = im_f
        s_scr[2, kb, :] = re_b
        s_scr[3, kb, :] = im_b
        return (ar_f * re_f - ai_f * im_f + z_scr[0, kf, :], ar_f * im_f + ai_f * re_f + z_scr[1, kf, :],
                ar_b * re_b - ai_b * im_b + z_scr[2, kb, :], ar_b * im_b + ai_b * re_b + z_scr[3, kb, :])

    zero = jnp.zeros((bsz, LANE), F32)
    lax.fori_loop(0, n_chunks, step, (zero, zero, zero, zero), unroll=2)
    y = jnp.dot(u, m_ref[...], preferred_element_type=F32)
    states = jnp.concatenate([s_scr[slab] for slab in range(4)], axis=1).astype(BF16)
    y += lax.dot_general(states, vt_ref[...], (((1,), (1,)), ((), ())), preferred_element_type=F32)
    y_ref[...] = y


def _s5(u_g, wz, mm, vt, at, bsz):
    groups, rows, width = u_g.shape
    n_chunks = rows // bsz
    sw = 4 * LANE
    return pl.pallas_call(
        functools.partial(_s5_kernel, n_chunks=n_chunks, bsz=bsz),
        grid=(groups,),
        in_specs=[
            pl.BlockSpec((None, rows, width), lambda g: (g, 0, 0)),
            pl.BlockSpec((None, width, sw), lambda g: (g, 0, 0)),
            pl.BlockSpec((None, width, width), lambda g: (g, 0, 0)),
            pl.BlockSpec((None, width, sw), lambda g: (g, 0, 0)),
            pl.BlockSpec((None, SUBLANE, LANE), lambda g: (g, 0, 0)),
        ],
        out_specs=pl.BlockSpec((None, rows, width), lambda g: (g, 0, 0)),
        out_shape=jax.ShapeDtypeStruct((groups, rows, width), F32),
        scratch_shapes=[pltpu.VMEM((4, rows, LANE), F32), pltpu.VMEM((4, rows, LANE), F32)],
        compiler_params=_cparams("parallel"),
        name="s5",
    )(u_g, wz, mm, vt, at)


PIECE = SSM_GROUP_CH
PIECES = LANE // PIECE
GROUP_TILES = SSM_CH // LANE


def _piece_gather(load_tile, shifts_and_tiles):
    acc = None
    for slot, (tile_idx, shift) in enumerate(shifts_and_tiles):
        src = load_tile(tile_idx)
        if shift % LANE:
            src = pltpu.roll(src, shift % LANE, axis=1)
        if acc is None:
            acc = src
        else:
            piece = lax.broadcasted_iota(jnp.int32, src.shape, 1) // PIECE
            acc = jnp.where(piece == slot, src, acc)
    return acc


def _s5_pack_kernel(z_ref, o_ref):
    for g in range(SSM_GROUPS):
        q = g % PIECES
        for jt in range(SSM_CHUNK // PIECES):
            plan = [((jt * PIECES + jj) * GROUP_TILES + g // PIECES, (jj - q) * PIECE) for jj in range(PIECES)]
            tile = _piece_gather(lambda i: z_ref[:, i * LANE:(i + 1) * LANE], plan)
            o_ref[g, :, jt * LANE:(jt + 1) * LANE] = tile.astype(BF16)


def _s5_unpack_kernel(y_ref, o_ref):
    for j in range(SSM_CHUNK):
        jj = j % PIECES
        for half in range(GROUP_TILES):
            plan = [((half * PIECES + q), (q - jj) * PIECE) for q in range(PIECES)]
            jt = j // PIECES
            tile = _piece_gather(lambda g: y_ref[g, :, jt * LANE:(jt + 1) * LANE], plan)
            o_ref[:, (j * GROUP_TILES + half) * LANE:(j * GROUP_TILES + half + 1) * LANE] = tile.astype(BF16)


def _s5_pack(uc):
    bsz, seq, ch = uc.shape
    n_chunks = seq // SSM_CHUNK
    z = uc.reshape(bsz, n_chunks, SSM_CHUNK * ch)
    width = SSM_CHUNK * SSM_GROUP_CH
    return pl.pallas_call(
        _s5_pack_kernel,
        grid=(bsz,),
        in_specs=[pl.BlockSpec((None, n_chunks, SSM_CHUNK * ch), lambda b: (b, 0, 0))],
        out_specs=pl.BlockSpec((SSM_GROUPS, n_chunks, width), lambda b: (0, b, 0)),
        out_shape=jax.ShapeDtypeStruct((SSM_GROUPS, bsz * n_chunks, width), BF16),
        compiler_params=_cparams("parallel"),
        name="s5_pack",
    )(z)


def _s5_unpack(y_g, bsz):
    groups, rows, width = y_g.shape
    n_chunks = rows // bsz
    out = pl.pallas_call(
        _s5_unpack_kernel,
        grid=(bsz,),
        in_specs=[pl.BlockSpec((groups, n_chunks, width), lambda b: (0, b, 0))],
        out_specs=pl.BlockSpec((None, n_chunks, SSM_CHUNK * SSM_CH), lambda b: (b, 0, 0)),
        out_shape=jax.ShapeDtypeStruct((bsz, n_chunks, SSM_CHUNK * SSM_CH), BF16),
        compiler_params=_cparams("parallel"),
        name="s5_unpack",
    )(y_g)
    return out.reshape(bsz, n_chunks * SSM_CHUNK, SSM_CH)


def _s5_matrices(a_re, a_im, log_dt, b_re, b_im, c_re, c_im, d_skip):
    t = SSM_CHUNK
    g, p, cg = SSM_GROUPS, SSM_STATE, SSM_GROUP_CH
    a = lax.complex(a_re.astype(F32), a_im.astype(F32))
    dt = jnp.exp(log_dt.astype(F32))[..., None]
    lam = a * dt
    a_bar = jnp.exp(lam)
    b_bar = ((a_bar - 1.0) / a)[..., None] * lax.complex(b_re.astype(F32), b_im.astype(F32))
    c = lax.complex(c_re.astype(F32), c_im.astype(F32))
    n = jnp.arange(t + 1, dtype=F32)
    pw = jnp.exp(lam[:, None] * n[None, :, None, None].astype(jnp.complex64))
    kern = jnp.einsum("dgcp,dtgp,dgpe->dtgce", c, pw[:, :t], b_bar).real
    eye = jnp.eye(cg, dtype=F32) * d_skip.astype(F32).reshape(g, cg)[:, :, None]
    lags = jnp.concatenate([kern[1][1:][::-1], (kern[0][0] + kern[1][0] + eye)[None], kern[0][1:]], 0)
    strip = lags.transpose(1, 3, 0, 2).reshape(g, cg, (2 * t - 1) * cg)
    mm = jnp.stack([strip[:, :, (t - 1 - j) * cg:(2 * t - 1 - j) * cg] for j in range(t)], axis=1)
    mm = mm.reshape(g, t * cg, t * cg)
    def lane_padded(v):
        return jnp.pad(v, [(0, 0)] * (v.ndim - 1) + [(0, LANE - p)])

    pw_g = pw.transpose(0, 2, 1, 3)
    pw_re, pw_im = lane_padded(pw_g.real), lane_padded(pw_g.imag)
    b_t = b_bar.transpose(0, 1, 3, 2)
    bt_re, bt_im = lane_padded(b_t.real), lane_padded(b_t.imag)
    c_re_p, c_im_p = lane_padded(c.real), lane_padded(c.imag)

    def slabs(a_re_, a_im_, m_re, m_im, sign):
        a_re_, a_im_ = a_re_[:, :, None, :], a_im_[:, :, None, :]
        m_re, m_im = m_re[:, None], m_im[:, None]
        re = a_re_ * m_re - a_im_ * m_im
        im = a_re_ * m_im + a_im_ * m_re
        return [re.reshape(g, t * cg, LANE), (sign * im).reshape(g, t * cg, LANE)]

    wz = jnp.concatenate(
        slabs(pw_re[0, :, :t][:, ::-1], pw_im[0, :, :t][:, ::-1], bt_re[0], bt_im[0], 1.0)
        + slabs(pw_re[1, :, :t], pw_im[1, :, :t], bt_re[1], bt_im[1], 1.0), axis=-1)
    vt = jnp.concatenate(
        slabs(pw_re[0, :, 1:t + 1], pw_im[0, :, 1:t + 1], c_re_p[0], c_im_p[0], -1.0)
        + slabs(pw_re[1, :, 1:t + 1][:, ::-1], pw_im[1, :, 1:t + 1][:, ::-1], c_re_p[1], c_im_p[1], -1.0),
        axis=-1)
    at_c = pw[:, t]
    lane_pad = jnp.zeros((g, LANE - p), F32)
    rows = [jnp.concatenate([at_c[0].real, lane_pad], -1), jnp.concatenate([at_c[0].imag, lane_pad], -1),
            jnp.concatenate([at_c[1].real, lane_pad], -1), jnp.concatenate([at_c[1].imag, lane_pad], -1)]
    at = jnp.stack(rows + [jnp.zeros((g, LANE), F32)] * (SUBLANE - 4), axis=1)
    return wz.astype(BF16), mm.astype(BF16), vt.astype(BF16), at


def _mix_kernel(oa_ref, ob_ref, y_ref, x_ref, g1_ref, wglu_ref, bglu_ref, gm_ref, wo_ref, o_ref):
    y = y_ref[...].astype(F32)
    y = 0.5 * y * (1.0 + jnp.tanh(math.sqrt(2.0 / math.pi) * (y + 0.044715 * (y * y * y))))
    z = _bdot(y, wglu_ref[...]) + bglu_ref[...]
    gate = z[:, SSM_CH:]
    oc = z[:, :SSM_CH] * (1.0 / (1.0 + jnp.exp(-gate)))
    oa, ob = oa_ref[...].astype(F32), ob_ref[...].astype(F32)
    na = oa * _rms_scale(oa, WIDTH_A) * gm_ref[:, 0:WIDTH_A]
    nb = ob * _rms_scale(ob, WIDTH_B) * gm_ref[:, WIDTH_A:WIDTH_A + WIDTH_B]
    nc = oc * _rms_scale(oc, SSM_CH) * gm_ref[:, WIDTH_A + WIDTH_B:]
    acc = _bdot(na, wo_ref[0:WIDTH_A, :])
    acc += _bdot(nb, wo_ref[WIDTH_A:WIDTH_A + WIDTH_B, :])
    acc += _bdot(nc, wo_ref[WIDTH_A + WIDTH_B:, :])
    o_ref[...] = x_ref[...] + g1_ref[...] * acc


def _mix(oa, ob, y, x, g1, wglu, bglu, gm, wo):
    bsz, seq, d = x.shape
    tm = TOKEN_TILE
    tok = lambda w: pl.BlockSpec((None, tm, w), lambda b, t: (b, t, 0))
    per_b = pl.BlockSpec((None, 1, d), lambda b, t: (b, 0, 0))
    return pl.pallas_call(
        _mix_kernel,
        grid=(bsz, seq // tm),
        in_specs=[tok(WIDTH_A), tok(WIDTH_B), tok(SSM_CH), tok(d), per_b, _resident(wglu.shape),
                  _resident(bglu.shape), _resident(gm.shape), _resident(wo.shape)],
        out_specs=tok(d),
        out_shape=jax.ShapeDtypeStruct((bsz, seq, d), F32),
        compiler_params=_cparams("parallel", "parallel"),
        name="mix",
    )(oa, ob, y, x, g1, wglu, bglu, gm, wo)


def _ffn_kernel(x_ref, xp_ref, xn_ref, sc_ref, sh_ref, g2_ref, n2_ref, wup_ref, cw_ref, cb_ref, wdn_ref,
                o_ref, h_scr, z_scr, act_scr, *, n_tiles):
    tm = x_ref.shape[0]
    t = pl.program_id(1)
    mod_scale = n2_ref[...] * (1.0 + sc_ref[...])
    shift = sh_ref[...]

    def normed(v):
        return v * _rms_scale(v, D_MODEL) * mod_scale + shift

    x = x_ref[...]
    not_first = (t > 0).astype(F32)
    not_last = (t < n_tiles - 1).astype(F32)
    h_scr[0:SUBLANE, :] = normed(xp_ref[...]) * not_first
    h_scr[SUBLANE:SUBLANE + tm, :] = normed(x)
    h_scr[SUBLANE + tm:, :] = normed(xn_ref[...]) * not_last
    h = h_scr[...].astype(BF16)
    fc = FFN_CHUNK

    def conv_cols(slot, col0):
        z_scr[slot] = jnp.dot(h, wup_ref[:, col0:col0 + fc], preferred_element_type=F32)
        cw = cw_ref[:, col0:col0 + fc]
        prev = z_scr[slot, SUBLANE - 1:SUBLANE - 1 + tm, :]
        cur = z_scr[slot, SUBLANE:SUBLANE + tm, :]
        nxt = z_scr[slot, SUBLANE + 1:SUBLANE + 1 + tm, :]
        return prev * cw[0:1, :] + cur * cw[1:2, :] + nxt * cw[2:3, :] + cb_ref[:, col0:col0 + fc]

    for f in range(FFN_HIDDEN // fc):
        val = conv_cols(0, f * fc)
        gate = conv_cols(1, FFN_HIDDEN + f * fc)
        act_scr[:, f * fc:(f + 1) * fc] = (gate * (1.0 / (1.0 + jnp.exp(-gate))) * val).astype(BF16)
    down = jnp.dot(act_scr[...], wdn_ref[...], preferred_element_type=F32)
    o_ref[...] = x + g2_ref[...] * down


def _ffn(x, sc2, sh2, g2, n2, wup, cw, cb, wdn):
    bsz, seq, d = x.shape
    tm = FFN_TILE
    n_tiles = seq // tm
    rows8 = tm // SUBLANE
    tok = pl.BlockSpec((None, tm, d), lambda b, t: (b, t, 0))
    prev8 = pl.BlockSpec((None, SUBLANE, d), lambda b, t: (b, jnp.maximum(t * rows8 - 1, 0), 0))
    next8 = pl.BlockSpec((None, SUBLANE, d),
                         lambda b, t: (b, jnp.minimum((t + 1) * rows8, seq // SUBLANE - 1), 0))
    per_b = pl.BlockSpec((None, 1, d), lambda b, t: (b, 0, 0))
    return pl.pallas_call(
        functools.partial(_ffn_kernel, n_tiles=n_tiles),
        grid=(bsz, n_tiles),
        in_specs=[tok, prev8, next8, per_b, per_b, per_b, _resident(n2.shape), _resident(wup.shape),
                  _resident(cw.shape), _resident(cb.shape), _resident(wdn.shape)],
        out_specs=tok,
        out_shape=jax.ShapeDtypeStruct((bsz, seq, d), F32),
        scratch_shapes=[pltpu.VMEM((tm + 2 * SUBLANE, d), F32),
                        pltpu.VMEM((2, tm + 2 * SUBLANE, FFN_CHUNK), F32),
                        pltpu.VMEM((tm, FFN_HIDDEN), BF16)],
        compiler_params=_cparams("parallel", "parallel"),
        name="ffn",
    )(x, x, x, sc2, sh2, g2, n2, wup, cw, cb, wdn)


def _swap_halves(w, heads, dim):
    half = dim // 2
    parts = []
    for hd in range(heads):
        parts += [w[..., hd * dim + half:(hd + 1) * dim], w[..., hd * dim:hd * dim + half]]
    return jnp.concatenate(parts, axis=-1)


def _prep_inproj(w_in, q_gain, w_uq, kv_gain, w_ukv, mla_gain, dil_gain):
    d = w_in.shape[0]
    z = lambda n: jnp.zeros((d, n), F32)
    a, b = w_in[:, :IN_A], w_in[:, IN_A:IN_A + IN_B]
    rope = a[:, MLA_Q_RANK + MLA_KV_RANK:]
    dq, dk, dv = b[:, :WIDTH_B], b[:, WIDTH_B:2 * WIDTH_B], b[:, 2 * WIDTH_B:]
    win = jnp.concatenate([
        a[:, :MLA_Q_RANK], z(Q_RANK_PAD - MLA_Q_RANK),
        a[:, MLA_Q_RANK:MLA_Q_RANK + MLA_KV_RANK],
        z(MLA_NOPE), rope, z(LANE - MLA_QK),
        z(MLA_NOPE), _swap_halves(rope, 1, MLA_ROPE), z(LANE - MLA_QK),
        dq, _swap_halves(dq, DIL_HEADS, DIL_HEAD_DIM),
        dk, _swap_halves(dk, DIL_HEADS, DIL_HEAD_DIM),
        dv, w_in[:, IN_A + IN_B:]], axis=1).astype(BF16)

    wq = w_uq.reshape(MLA_Q_RANK, MLA_HEADS, MLA_QK)
    zq = lambda n: jnp.zeros((MLA_Q_RANK, MLA_HEADS, n), F32)
    slab = jnp.concatenate([wq, zq(LANE - MLA_QK)], -1)
    slab_s = jnp.concatenate([zq(MLA_NOPE), wq[..., MLA_NOPE + MLA_ROPE // 2:],
                              wq[..., MLA_NOPE:MLA_NOPE + MLA_ROPE // 2], zq(LANE - MLA_QK)], -1)
    wuq = jnp.concatenate([slab.reshape(MLA_Q_RANK, QA_W), slab_s.reshape(MLA_Q_RANK, QA_W)], 1)
    wuq = jnp.pad(wuq, ((0, Q_RANK_PAD - MLA_Q_RANK), (0, 0))).astype(BF16)
    qg = jnp.pad(q_gain, (0, Q_RANK_PAD - MLA_Q_RANK)).reshape(1, Q_RANK_PAD)

    wkv = w_ukv.reshape(MLA_KV_RANK, MLA_HEADS, MLA_NOPE + MLA_V)
    k_slab = jnp.concatenate([wkv[..., :MLA_NOPE], jnp.zeros((MLA_KV_RANK, MLA_HEADS, LANE - MLA_NOPE), F32)], -1)
    wukv = jnp.concatenate([k_slab.reshape(MLA_KV_RANK, QA_W),
                            wkv[..., MLA_NOPE:].reshape(MLA_KV_RANK, WIDTH_A)], 1).astype(BF16)
    kvg = kv_gain.reshape(1, MLA_KV_RANK)

    def mla_rows(g):
        pad = jnp.zeros((LANE - MLA_QK,), F32)
        plain = jnp.concatenate([g, pad])
        swapped = jnp.concatenate([jnp.zeros((MLA_NOPE,), F32), g[MLA_NOPE + MLA_ROPE // 2:],
                                   g[MLA_NOPE:MLA_NOPE + MLA_ROPE // 2], pad])
        return [plain, swapped]

    def dil_rows(g):
        return [jnp.tile(g, 2), jnp.tile(_swap_halves(g, 1, DIL_HEAD_DIM), 2)]

    fill = [jnp.zeros((LANE,), F32)] * (SUBLANE - 4)
    a_gains = jnp.stack(mla_rows(mla_gain[0]) + mla_rows(mla_gain[1]) + fill)
    b_gains = jnp.stack(dil_rows(dil_gain[0]) + dil_rows(dil_gain[1]) + fill)
    return win, qg, wuq, kvg, wukv, a_gains, b_gains


def _rope_tables(positions):
    def tables(dim):
        inv_freq = 1.0 / (ROPE_THETA ** (jnp.arange(0, dim, 2, dtype=F32) / dim))
        ang = positions.astype(F32)[..., None] * inv_freq
        return jnp.cos(ang), jnp.sin(ang)

    cos_r, sin_r = tables(MLA_ROPE)
    cos_f, sin_f = tables(DIL_HEAD_DIM)
    lead = positions.shape
    ones = jnp.ones(lead + (MLA_NOPE,), F32)
    zeros = jnp.zeros(lead + (MLA_NOPE,), F32)
    pad = jnp.zeros(lead + (LANE - MLA_QK,), F32)
    ca = jnp.concatenate([ones, cos_r, cos_r, pad], -1)
    sa = jnp.concatenate([zeros, -sin_r, sin_r, pad], -1)
    cb = jnp.concatenate([cos_f, cos_f, cos_f, cos_f], -1)
    sb = jnp.concatenate([-sin_f, sin_f, -sin_f, sin_f], -1)
    return ca, sa, cb, sb


def kernel(x, c, positions, w_mod, b_mod, norm1, w_in, mla_q_norm, mla_w_uq, mla_kv_norm, mla_w_ukv, mla_qk_gain, dil_qk_gain, ssm_a_re, ssm_a_im, ssm_log_dt, ssm_b_re, ssm_b_im, ssm_c_re, ssm_c_im, ssm_d, ssm_w_glu, ssm_b_glu, mix_norm, w_out, norm2, ffn_w_up, ffn_conv_w, ffn_conv_b, ffn_w_down):
    bsz, seq, d = x.shape
    depth = w_mod.shape[0]
    assert d == D_MODEL and seq % TOKEN_TILE == 0 and seq % SSM_CHUNK == 0
    tabs = _rope_tables(positions)
    mod = _modulation(c, w_mod, b_mod)
    for l in range(depth):
        sh1, sc1, g1, sh2, sc2, g2 = (m.reshape(bsz, 1, d) for m in jnp.split(mod[l], N_MOD, axis=-1))
        win, qg, wuq, kvg, wukv, a_gains, b_gains = _prep_inproj(
            w_in[l], mla_q_norm[l], mla_w_uq[l], mla_kv_norm[l], mla_w_ukv[l], mla_qk_gain[l], dil_qk_gain[l])
        qa, ka, va, qb, kb, vb, uc = _inproj(x, sc1, sh1, norm1[l].reshape(1, d), win, qg, wuq, kvg, wukv,
                                             a_gains, b_gains, tabs)
        oa = _attention(qa, ka, va)
        ob = _dilated_attention(qb, kb, vb)
        wz, mm, vt, at = _s5_matrices(ssm_a_re[l], ssm_a_im[l], ssm_log_dt[l], ssm_b_re[l], ssm_b_im[l],
                                      ssm_c_re[l], ssm_c_im[l], ssm_d[l])
        y = _s5_unpack(_s5(_s5_pack(uc), wz, mm, vt, at, bsz), bsz)
        x = _mix(oa, ob, y, x, g1, ssm_w_glu[l].astype(BF16), ssm_b_glu[l].reshape(1, -1),
                 mix_norm[l].reshape(1, -1), w_out[l].astype(BF16))
        x = _ffn(x, sc2, sh2, g2, norm2[l].reshape(1, d), ffn_w_up[l].astype(BF16), ffn_conv_w[l],
                 ffn_conv_b[l].reshape(1, -1), ffn_w_down[l].astype(BF16))
    return x
```

```python
import functools
import math

import jax
import jax.numpy as jnp
from jax import lax
from jax.experimental import pallas as pl
from jax.experimental.pallas import tpu as pltpu

F32 = jnp.float32
BF16 = jnp.bfloat16

D_MODEL = 1024
MLA_HEADS = 6
MLA_NOPE = 64
MLA_ROPE = 32
MLA_V = 64
MLA_QK = MLA_NOPE + MLA_ROPE
MLA_Q_RANK = 192
MLA_KV_RANK = 128
DIL_HEADS = 6
DIL_HEAD_DIM = 64
DIL_PATTERNS = ((128, 1), (512, 4), (2048, 16))
SSM_GROUPS = 16
SSM_GROUP_CH = 16
SSM_CH = SSM_GROUPS * SSM_GROUP_CH
SSM_STATE = 64
WIDTH_A = MLA_HEADS * MLA_V
WIDTH_B = DIL_HEADS * DIL_HEAD_DIM
MIX_WIDTH = WIDTH_A + WIDTH_B + SSM_CH
IN_A = MLA_Q_RANK + MLA_KV_RANK + MLA_ROPE
IN_B = 3 * WIDTH_B
FFN_HIDDEN = 2816
ROPE_THETA = 10000.0
EPS = 1e-6
NEG_INF = -1e30
N_MOD = 6

LANE = 128
SUBLANE = 8
HALF = LANE // 2
Q_RANK_PAD = 256
TOKEN_TILE = 512
FFN_TILE = 1024
ATTN_Q_TILE = 1024
ATTN_PV_SPLIT = 2
DIL_Q_BLOCK = 128
DIL_GROUP = 4
FFN_CHUNK = 256
SSM_CHUNK = 32
VMEM_LIMIT = 56 * 1024 * 1024

C_CQ = 0
C_CKV = C_CQ + Q_RANK_PAD
C_KR = C_CKV + MLA_KV_RANK
C_KRS = C_KR + LANE
C_DQ = C_KRS + LANE
C_DQS = C_DQ + WIDTH_B
C_DK = C_DQS + WIDTH_B
C_DKS = C_DK + WIDTH_B
C_DV = C_DKS + WIDTH_B
C_UC = C_DV + WIDTH_B
IN_EXT = C_UC + SSM_CH
N_PAIRS = MLA_HEADS // 2
QA_W = MLA_HEADS * LANE


def _cparams(*sem):
    return pltpu.CompilerParams(dimension_semantics=sem, vmem_limit_bytes=VMEM_LIMIT)


def _resident(shape):
    nd = len(shape)
    return pl.BlockSpec(shape, lambda *_: (0,) * nd, pipeline_mode=pl.Buffered(1))


def _rms_scale(v, width):
    return lax.rsqrt(jnp.sum(v * v, axis=-1, keepdims=True) * (1.0 / width) + EPS)


def _bdot(a, b):
    return jnp.dot(a.astype(BF16), b, preferred_element_type=F32)


def _mod_kernel(c_ref, w_ref, b_ref, o_ref):
    c = c_ref[...]
    ca = c * (1.0 / (1.0 + jnp.exp(-c)))
    w = w_ref[...]
    c_hi = ca.astype(BF16)
    c_lo = (ca - c_hi.astype(F32)).astype(BF16)
    w_hi = w.astype(BF16)
    w_lo = (w - w_hi.astype(F32)).astype(BF16)
    acc = jnp.dot(c_hi, w_hi, preferred_element_type=F32)
    acc += jnp.dot(c_lo, w_hi, preferred_element_type=F32)
    acc += jnp.dot(c_hi, w_lo, preferred_element_type=F32)
    o_ref[...] = acc + b_ref[...]


def _modulation(c, w_mod, b_mod):
    depth, d, n = w_mod.shape
    bsz = c.shape[0]
    tn = 1024
    return pl.pallas_call(
        _mod_kernel,
        grid=(depth, n // tn),
        in_specs=[
            pl.BlockSpec((bsz, d), lambda l, j: (0, 0)),
            pl.BlockSpec((None, d, tn), lambda l, j: (l, 0, j)),
            pl.BlockSpec((None, 1, tn), lambda l, j: (l, 0, j)),
        ],
        out_specs=pl.BlockSpec((None, bsz, tn), lambda l, j: (l, 0, j)),
        out_shape=jax.ShapeDtypeStruct((depth, bsz, n), F32),
        compiler_params=_cparams("parallel", "parallel"),
        name="modulation",
    )(c, w_mod, b_mod.reshape(depth, 1, n))


def _inproj_kernel(x_ref, sc_ref, sh_ref, n1_ref, win_ref, qg_ref, wuq_ref, kvg_ref, wukv_ref,
                   ag_ref, bg_ref, ca_ref, sa_ref, cb_ref, sb_ref,
                   qa_ref, ka_ref, va_ref, qb_ref, kb_ref, vb_ref, *uc_refs):
    x = x_ref[...]
    h = x * _rms_scale(x, D_MODEL) * n1_ref[...]
    h = h * (1.0 + sc_ref[...]) + sh_ref[...]
    u = _bdot(h, win_ref[...])

    cq = u[:, C_CQ:C_CQ + Q_RANK_PAD]
    cqn = cq * _rms_scale(cq, MLA_Q_RANK) * qg_ref[...]
    q_all = _bdot(cqn, wuq_ref[...])
    ckv = u[:, C_CKV:C_CKV + MLA_KV_RANK]
    ckvn = ckv * _rms_scale(ckv, MLA_KV_RANK) * kvg_ref[...]
    kv_all = _bdot(ckvn, wukv_ref[...])
    kr = u[:, C_KR:C_KR + LANE]
    krs = u[:, C_KRS:C_KRS + LANE]

    ca, sa = ca_ref[...], sa_ref[...]
    gq, gqs, gk, gks = ag_ref[0:1, :], ag_ref[1:2, :], ag_ref[2:3, :], ag_ref[3:4, :]
    cos_q, sin_q = ca * gq, sa * gqs
    cos_k, sin_k = ca * gk, sa * gks
    krs_rot = krs * sin_k
    a_scale = MLA_QK ** -0.5
    for hd in range(MLA_HEADS):
        q = q_all[:, hd * LANE:(hd + 1) * LANE]
        qs = q_all[:, QA_W + hd * LANE:QA_W + (hd + 1) * LANE]
        r = _rms_scale(q, MLA_QK) * a_scale
        qa_ref[:, hd * LANE:(hd + 1) * LANE] = ((q * cos_q + qs * sin_q) * r).astype(BF16)
        k = kv_all[:, hd * LANE:(hd + 1) * LANE] + kr
        r = _rms_scale(k, MLA_QK)
        ka_ref[:, hd * LANE:(hd + 1) * LANE] = ((k * cos_k + krs_rot) * r).astype(BF16)
    va_ref[...] = kv_all[:, QA_W:QA_W + WIDTH_A].astype(BF16)

    cb, sb = cb_ref[...], sb_ref[...]
    gq, gqs, gk, gks = bg_ref[0:1, :], bg_ref[1:2, :], bg_ref[2:3, :], bg_ref[3:4, :]
    cos_q, sin_q = cb * gq, sb * gqs
    cos_k, sin_k = cb * gk, sb * gks
    low = lax.broadcasted_iota(jnp.int32, (x.shape[0], LANE), 1) < HALF
    b_scale = DIL_HEAD_DIM ** -0.5

    def pair_scale(v):
        sq = v * v
        s_all = jnp.sum(sq, axis=-1, keepdims=True)
        s_low = jnp.sum(jnp.where(low, sq, 0.0), axis=-1, keepdims=True)
        r_low = lax.rsqrt(s_low * (1.0 / DIL_HEAD_DIM) + EPS)
        r_high = lax.rsqrt((s_all - s_low) * (1.0 / DIL_HEAD_DIM) + EPS)
        return jnp.where(low, r_low, r_high)

    for p in range(N_PAIRS):
        q = u[:, C_DQ + p * LANE:C_DQ + (p + 1) * LANE]
        qs = u[:, C_DQS + p * LANE:C_DQS + (p + 1) * LANE]
        qo = (q * cos_q + qs * sin_q) * (pair_scale(q) * b_scale)
        qb_ref[:, 2 * p * LANE:(2 * p + 1) * LANE] = jnp.where(low, qo, 0.0).astype(BF16)
        qb_ref[:, (2 * p + 1) * LANE:(2 * p + 2) * LANE] = jnp.where(low, 0.0, qo).astype(BF16)
        k = u[:, C_DK + p * LANE:C_DK + (p + 1) * LANE]
        ks = u[:, C_DKS + p * LANE:C_DKS + (p + 1) * LANE]
        kb_ref[:, p * LANE:(p + 1) * LANE] = ((k * cos_k + ks * sin_k) * pair_scale(k)).astype(BF16)
    vb_ref[...] = u[:, C_DV:C_DV + WIDTH_B].astype(BF16)
    for i, uc_ref in enumerate(uc_refs):
        uc_ref[...] = u[:, C_UC + i * LANE:C_UC + (i + 1) * LANE]


def _inproj(x, sc1, sh1, n1, win, qg, wuq, kvg, wukv, a_gains, b_gains, tabs):
    bsz, seq, d = x.shape
    tm = TOKEN_TILE
    tok = lambda w: pl.BlockSpec((None, tm, w), lambda b, t: (b, t, 0))
    per_b = pl.BlockSpec((None, 1, d), lambda b, t: (b, 0, 0))
    outs = [(QA_W, BF16), (QA_W, BF16), (WIDTH_A, BF16), (2 * WIDTH_B, BF16), (WIDTH_B, BF16),
            (WIDTH_B, BF16)] + [(LANE, F32)] * (SSM_CH // LANE)
    return pl.pallas_call(
        _inproj_kernel,
        grid=(bsz, seq // tm),
        in_specs=[tok(d), per_b, per_b, _resident((1, d)), _resident(win.shape), _resident(qg.shape),
                  _resident(wuq.shape), _resident(kvg.shape), _resident(wukv.shape),
                  _resident(a_gains.shape), _resident(b_gains.shape),
                  tok(LANE), tok(LANE), tok(LANE), tok(LANE)],
        out_specs=[tok(w) for w, _ in outs],
        out_shape=[jax.ShapeDtypeStruct((bsz, seq, w), dt) for w, dt in outs],
        compiler_params=_cparams("parallel", "parallel"),
        name="inproj",
    )(x, sc1, sh1, n1, win, qg, wuq, kvg, wukv, a_gains, b_gains, *tabs)


def _attn_kernel(q_ref, k_ref, v_ref, o_ref):
    v = v_ref[...]
    low_k = lax.broadcasted_iota(jnp.int32, v.shape, 1) < HALF
    outs = []
    for j in range(2):
        q = q_ref[:, j * LANE:(j + 1) * LANE]
        k = k_ref[:, j * LANE:(j + 1) * LANE]
        s = lax.dot_general(q, k, (((1,), (1,)), ((), ())), preferred_element_type=F32)
        m = jnp.max(s, axis=-1, keepdims=True)
        pb = jnp.exp(s - m).astype(BF16)
        vb = jnp.where(low_k == (j == 0), v, jnp.ones_like(v))
        rows = pb.shape[0] // ATTN_PV_SPLIT
        o = jnp.concatenate([jnp.dot(pb[i * rows:(i + 1) * rows], vb, preferred_element_type=F32)
                             for i in range(ATTN_PV_SPLIT)], axis=0)
        outs.append(o * (1.0 / pltpu.roll(o, HALF, axis=1)))
    low = lax.broadcasted_iota(jnp.int32, outs[0].shape, 1) < HALF
    o_ref[...] = jnp.where(low, outs[0], outs[1]).astype(o_ref.dtype)


def _attention(q, k, v):
    bsz, seq, _ = q.shape
    tq = ATTN_Q_TILE
    return pl.pallas_call(
        _attn_kernel,
        grid=(bsz, N_PAIRS, seq // tq),
        in_specs=[
            pl.BlockSpec((None, tq, 2 * LANE), lambda b, p, i: (b, i, p)),
            pl.BlockSpec((None, seq, 2 * LANE), lambda b, p, i: (b, 0, p)),
            pl.BlockSpec((None, seq, LANE), lambda b, p, i: (b, 0, p)),
        ],
        out_specs=pl.BlockSpec((None, tq, LANE), lambda b, p, i: (b, i, p)),
        out_shape=jax.ShapeDtypeStruct((bsz, seq, N_PAIRS * LANE), BF16),
        compiler_params=_cparams("parallel", "parallel", "parallel"),
        name="mla_attention",
    )(q, k, v)


def _dilated_kernel(q_ref, k_ref, v_ref, o_ref, qf, kf, vf, acc, mx):
    seq = k_ref.shape[0]
    kf[...] = k_ref[...].astype(F32)
    vf[...] = v_ref[...].astype(F32)
    for j in range(2):
        qf[j] = q_ref[:, j * LANE:(j + 1) * LANE].astype(F32)
    qb = DIL_Q_BLOCK
    for branch, (window, dil) in enumerate(DIL_PATTERNS):
        half = window // (2 * dil)
        n_sub = seq // dil
        kw = min(n_sub, qb + 2 * half)
        key_minus_query = (lax.broadcasted_iota(jnp.int32, (qb, kw), 1)
                           - lax.broadcasted_iota(jnp.int32, (qb, kw), 0))
        low_k = lax.broadcasted_iota(jnp.int32, (kw, LANE), 1) < HALF
        blocks = [(res, j0) for res in range(dil) for j0 in range(0, n_sub, qb)]
        for g0 in range(0, len(blocks), DIL_GROUP):
            chains = []
            for res, j0 in blocks[g0:g0 + DIL_GROUP]:
                k0 = min(max(j0 - half, 0), n_sub - kw)
                band = jnp.where(jnp.abs(key_minus_query + (k0 - j0)) <= half, 0.0, NEG_INF)
                q_rows = pl.ds(res + dil * j0, qb, stride=dil)
                k_rows = pl.ds(res + dil * k0, kw, stride=dil)
                kb = kf[k_rows, :].astype(BF16)
                v_win = vf[k_rows, :]
                for j in range(2):
                    vb = jnp.where(low_k == (j == 0), v_win, 1.0).astype(BF16)
                    q = qf[j, q_rows, :].astype(BF16)
                    s = lax.dot_general(q, kb, (((1,), (1,)), ((), ())), preferred_element_type=F32) + band
                    chains.append((j, q_rows, s, vb))
            probs = []
            for j, q_rows, s, vb in chains:
                m = jnp.max(s, axis=-1, keepdims=True)
                mx[branch, j, q_rows, :] = jnp.broadcast_to(m, (qb, LANE))
                probs.append(jnp.exp(s - m).astype(BF16))
            for (j, q_rows, s, vb), p in zip(chains, probs):
                acc[branch, j, q_rows, :] = jnp.dot(p, vb, preferred_element_type=F32)
    outs = []
    n_br = len(DIL_PATTERNS)
    for j in range(2):
        m_all = mx[0, j]
        for br in range(1, n_br):
            m_all = jnp.maximum(m_all, mx[br, j])
        tot = None
        for br in range(n_br):
            w = jnp.exp(mx[br, j] - m_all)
            tot = w * acc[br, j] if tot is None else tot + w * acc[br, j]
        outs.append(tot * (1.0 / pltpu.roll(tot, HALF, axis=1)))
    low = lax.broadcasted_iota(jnp.int32, (seq, LANE), 1) < HALF
    o_ref[...] = jnp.where(low, outs[0], outs[1]).astype(o_ref.dtype)


def _dilated_attention(q, k, v):
    bsz, seq, _ = q.shape
    assert all(seq % (dil * DIL_Q_BLOCK) == 0 for _, dil in DIL_PATTERNS)
    return pl.pallas_call(
        _dilated_kernel,
        grid=(bsz, N_PAIRS),
        in_specs=[
            pl.BlockSpec((None, seq, 2 * LANE), lambda b, p: (b, 0, p)),
            pl.BlockSpec((None, seq, LANE), lambda b, p: (b, 0, p)),
            pl.BlockSpec((None, seq, LANE), lambda b, p: (b, 0, p)),
        ],
        out_specs=pl.BlockSpec((None, seq, LANE), lambda b, p: (b, 0, p)),
        out_shape=jax.ShapeDtypeStruct((bsz, seq, N_PAIRS * LANE), BF16),
        scratch_shapes=[pltpu.VMEM((2, seq, LANE), F32), pltpu.VMEM((seq, LANE), F32),
                        pltpu.VMEM((seq, LANE), F32)]
        + [pltpu.VMEM((len(DIL_PATTERNS), 2, seq, LANE), F32)] * 2,
        compiler_params=_cparams("parallel", "parallel"),
        name="dilated_attention",
    )(q, k, v)


def _s5_kernel(u_ref, wz_ref, m_ref, vt_ref, at_ref, y_ref, z_scr, s_scr, *, n_chunks, bsz):
    u = u_ref[...]
    z = jnp.dot(u, wz_ref[...], preferred_element_type=F32)
    for slab in range(4):
        z_scr[slab] = z[:, slab * LANE:(slab + 1) * LANE]
    at = at_ref[...]
    ar_f, ai_f, ar_b, ai_b = at[0:1, :], at[1:2, :], at[2:3, :], at[3:4, :]

    def chunk_rows(kk):
        return pl.ds(kk, bsz, stride=n_chunks)

    def step(i, carry):
        re_f, im_f, re_b, im_b = carry
        kf, kb = chunk_rows(i), chunk_rows(n_chunks - 1 - i)
        s_scr[0, kf, :] = re_f
        s_scr[1, kf, :] = im_f
        s_scr[2, kb, :] = re_b
        s_scr[3, kb, :] = im_b
        return (ar_f * re_f - ai_f * im_f + z_scr[0, kf, :], ar_f * im_f + ai_f * re_f + z_scr[1, kf, :],
                ar_b * re_b - ai_b * im_b + z_scr[2, kb, :], ar_b * im_b + ai_b * re_b + z_scr[3, kb, :])

    zero = jnp.zeros((bsz, LANE), F32)
    lax.fori_loop(0, n_chunks, step, (zero, zero, zero, zero), unroll=2)
    y = jnp.dot(u, m_ref[...], preferred_element_type=F32)
    states = jnp.concatenate([s_scr[slab] for slab in range(4)], axis=1).astype(BF16)
    y += lax.dot_general(states, vt_ref[...], (((1,), (1,)), ((), ())), preferred_element_type=F32)
    y_ref[...] = y


def _s5(u_g, wz, mm, vt, at, bsz):
    groups, rows, width = u_g.shape
    n_chunks = rows // bsz
    sw = 4 * LANE
    return pl.pallas_call(
        functools.partial(_s5_kernel, n_chunks=n_chunks, bsz=bsz),
        grid=(groups,),
        in_specs=[
            pl.BlockSpec((None, rows, width), lambda g: (g, 0, 0)),
            pl.BlockSpec((None, width, sw), lambda g: (g, 0, 0)),
            pl.BlockSpec((None, width, width), lambda g: (g, 0, 0)),
            pl.BlockSpec((None, width, sw), lambda g: (g, 0, 0)),
            pl.BlockSpec((None, SUBLANE, LANE), lambda g: (g, 0, 0)),
        ],
        out_specs=pl.BlockSpec((None, rows, width), lambda g: (g, 0, 0)),
        out_shape=jax.ShapeDtypeStruct((groups, rows, width), F32),
        scratch_shapes=[pltpu.VMEM((4, rows, LANE), F32), pltpu.VMEM((4, rows, LANE), F32)],
        compiler_params=_cparams("parallel"),
        name="s5",
    )(u_g, wz, mm, vt, at)


PIECE = SSM_GROUP_CH
PIECES = LANE // PIECE
GROUP_TILES = SSM_CH // LANE


def _piece_gather(load_tile, shifts_and_tiles):
    acc = None
    for slot, (tile_idx, shift) in enumerate(shifts_and_tiles):
        src = load_tile(tile_idx)
        if shift % LANE:
            src = pltpu.roll(src, shift % LANE, axis=1)
        if acc is None:
            acc = src
        else:
            piece = lax.broadcasted_iota(jnp.int32, src.shape, 1) // PIECE
            acc = jnp.where(piece == slot, src, acc)
    return acc


def _s5_pack_kernel(*refs):
    z_refs, o_ref = refs[:GROUP_TILES], refs[GROUP_TILES]
    n_chunks = o_ref.shape[1]

    tiles = [z_refs[i % GROUP_TILES][pl.ds(i // GROUP_TILES, n_chunks, stride=SSM_CHUNK), :]
             for i in range(SSM_CHUNK * GROUP_TILES)]
    for g in range(SSM_GROUPS):
        q = g % PIECES
        for jt in range(SSM_CHUNK // PIECES):
            plan = [((jt * PIECES + jj) * GROUP_TILES + g // PIECES, (jj - q) * PIECE) for jj in range(PIECES)]
            o_ref[g, :, jt * LANE:(jt + 1) * LANE] = _piece_gather(lambda i: tiles[i], plan).astype(BF16)


def _s5_unpack_kernel(y_ref, *o_refs):
    n_chunks = y_ref.shape[1]
    for j in range(SSM_CHUNK):
        jj = j % PIECES
        jt = j // PIECES
        for half in range(GROUP_TILES):
            plan = [((half * PIECES + q), (q - jj) * PIECE) for q in range(PIECES)]
            tile = _piece_gather(lambda g: y_ref[g, :, jt * LANE:(jt + 1) * LANE], plan)
            o_refs[half][pl.ds(j, n_chunks, stride=SSM_CHUNK), :] = tile


def _s5_pack(uc_halves):
    bsz, seq, _ = uc_halves[0].shape
    n_chunks = seq // SSM_CHUNK
    width = SSM_CHUNK * SSM_GROUP_CH
    return pl.pallas_call(
        _s5_pack_kernel,
        grid=(bsz,),
        in_specs=[pl.BlockSpec((None, seq, LANE), lambda b: (b, 0, 0))] * GROUP_TILES,
        out_specs=pl.BlockSpec((SSM_GROUPS, n_chunks, width), lambda b: (0, b, 0)),
        out_shape=jax.ShapeDtypeStruct((SSM_GROUPS, bsz * n_chunks, width), BF16),
        compiler_params=_cparams("parallel"),
        name="s5_pack",
    )(*uc_halves)


def _s5_unpack(y_g, bsz):
    groups, rows, width = y_g.shape
    n_chunks = rows // bsz
    seq = n_chunks * SSM_CHUNK
    return pl.pallas_call(
        _s5_unpack_kernel,
        grid=(bsz,),
        in_specs=[pl.BlockSpec((groups, n_chunks, width), lambda b: (0, b, 0))],
        out_specs=[pl.BlockSpec((None, seq, LANE), lambda b: (b, 0, 0))] * GROUP_TILES,
        out_shape=[jax.ShapeDtypeStruct((bsz, seq, LANE), F32)] * GROUP_TILES,
        compiler_params=_cparams("parallel"),
        name="s5_unpack",
    )(y_g)


def _s5_matrices(a_re, a_im, log_dt, b_re, b_im, c_re, c_im, d_skip):
    t = SSM_CHUNK
    g, p, cg = SSM_GROUPS, SSM_STATE, SSM_GROUP_CH
    a = lax.complex(a_re.astype(F32), a_im.astype(F32))
    dt = jnp.exp(log_dt.astype(F32))[..., None]
    lam = a * dt
    a_bar = jnp.exp(lam)
    b_bar = ((a_bar - 1.0) / a)[..., None] * lax.complex(b_re.astype(F32), b_im.astype(F32))
    c = lax.complex(c_re.astype(F32), c_im.astype(F32))
    n = jnp.arange(t + 1, dtype=F32)
    pw = jnp.exp(lam[:, None] * n[None, :, None, None].astype(jnp.complex64))
    kern = jnp.einsum("dgcp,dtgp,dgpe->dtgce", c, pw[:, :t], b_bar).real
    eye = jnp.eye(cg, dtype=F32) * d_skip.astype(F32).reshape(g, cg)[:, :, None]
    lags = jnp.concatenate([kern[1][1:][::-1], (kern[0][0] + kern[1][0] + eye)[None], kern[0][1:]], 0)
    strip = lags.transpose(1, 3, 0, 2).reshape(g, cg, (2 * t - 1) * cg)
    mm = jnp.stack([strip[:, :, (t - 1 - j) * cg:(2 * t - 1 - j) * cg] for j in range(t)], axis=1)
    mm = mm.reshape(g, t * cg, t * cg)
    def lane_padded(v):
        return jnp.pad(v, [(0, 0)] * (v.ndim - 1) + [(0, LANE - p)])

    pw_g = pw.transpose(0, 2, 1, 3)
    pw_re, pw_im = lane_padded(pw_g.real), lane_padded(pw_g.imag)
    b_t = b_bar.transpose(0, 1, 3, 2)
    bt_re, bt_im = lane_padded(b_t.real), lane_padded(b_t.imag)
    c_re_p, c_im_p = lane_padded(c.real), lane_padded(c.imag)

    def slabs(a_re_, a_im_, m_re, m_im, sign):
        a_re_, a_im_ = a_re_[:, :, None, :], a_im_[:, :, None, :]
        m_re, m_im = m_re[:, None], m_im[:, None]
        re = a_re_ * m_re - a_im_ * m_im
        im = a_re_ * m_im + a_im_ * m_re
        return [re.reshape(g, t * cg, LANE), (sign * im).reshape(g, t * cg, LANE)]

    wz = jnp.concatenate(
        slabs(pw_re[0, :, :t][:, ::-1], pw_im[0, :, :t][:, ::-1], bt_re[0], bt_im[0], 1.0)
        + slabs(pw_re[1, :, :t], pw_im[1, :, :t], bt_re[1], bt_im[1], 1.0), axis=-1)
    vt = jnp.concatenate(
        slabs(pw_re[0, :, 1:t + 1], pw_im[0, :, 1:t + 1], c_re_p[0], c_im_p[0], -1.0)
        + slabs(pw_re[1, :, 1:t + 1][:, ::-1], pw_im[1, :, 1:t + 1][:, ::-1], c_re_p[1], c_im_p[1], -1.0),
        axis=-1)
    at_c = pw[:, t]
    lane_pad = jnp.zeros((g, LANE - p), F32)
    rows = [jnp.concatenate([at_c[0].real, lane_pad], -1), jnp.concatenate([at_c[0].imag, lane_pad], -1),
            jnp.concatenate([at_c[1].real, lane_pad], -1), jnp.concatenate([at_c[1].imag, lane_pad], -1)]
    at = jnp.stack(rows + [jnp.zeros((g, LANE), F32)] * (SUBLANE - 4), axis=1)
    return wz.astype(BF16), mm.astype(BF16), vt.astype(BF16), at


def _mix_kernel(oa_ref, ob_ref, y0_ref, y1_ref, x_ref, g1_ref, wglu_ref, bglu_ref, gm_ref, wo_ref, o_ref):
    y = jnp.concatenate([y0_ref[...], y1_ref[...]], axis=1)
    y = 0.5 * y * (1.0 + jnp.tanh(math.sqrt(2.0 / math.pi) * (y + 0.044715 * (y * y * y))))
    z = _bdot(y, wglu_ref[...]) + bglu_ref[...]
    gate = z[:, SSM_CH:]
    oc = z[:, :SSM_CH] * (1.0 / (1.0 + jnp.exp(-gate)))
    oa, ob = oa_ref[...].astype(F32), ob_ref[...].astype(F32)
    na = oa * _rms_scale(oa, WIDTH_A) * gm_ref[:, 0:WIDTH_A]
    nb = ob * _rms_scale(ob, WIDTH_B) * gm_ref[:, WIDTH_A:WIDTH_A + WIDTH_B]
    nc = oc * _rms_scale(oc, SSM_CH) * gm_ref[:, WIDTH_A + WIDTH_B:]
    acc = _bdot(na, wo_ref[0:WIDTH_A, :])
    acc += _bdot(nb, wo_ref[WIDTH_A:WIDTH_A + WIDTH_B, :])
    acc += _bdot(nc, wo_ref[WIDTH_A + WIDTH_B:, :])
    o_ref[...] = x_ref[...] + g1_ref[...] * acc


def _mix(oa, ob, y, x, g1, wglu, bglu, gm, wo):
    bsz, seq, d = x.shape
    tm = TOKEN_TILE
    tok = lambda w: pl.BlockSpec((None, tm, w), lambda b, t: (b, t, 0))
    per_b = pl.BlockSpec((None, 1, d), lambda b, t: (b, 0, 0))
    return pl.pallas_call(
        _mix_kernel,
        grid=(bsz, seq // tm),
        in_specs=[tok(WIDTH_A), tok(WIDTH_B), tok(LANE), tok(LANE), tok(d), per_b, _resident(wglu.shape),
                  _resident(bglu.shape), _resident(gm.shape), _resident(wo.shape)],
        out_specs=tok(d),
        out_shape=jax.ShapeDtypeStruct((bsz, seq, d), F32),
        compiler_params=_cparams("parallel", "parallel"),
        name="mix",
    )(oa, ob, *y, x, g1, wglu, bglu, gm, wo)


def _ffn_kernel(x_ref, xp_ref, xn_ref, sc_ref, sh_ref, g2_ref, n2_ref, wup_ref, cw_ref, cb_ref, wdn_ref,
                o_ref, h_scr, z_scr, act_scr, *, n_tiles):
    tm = x_ref.shape[0]
    t = pl.program_id(1)
    mod_scale = n2_ref[...] * (1.0 + sc_ref[...])
    shift = sh_ref[...]

    def normed(v):
        return v * _rms_scale(v, D_MODEL) * mod_scale + shift

    x = x_ref[...]
    not_first = (t > 0).astype(F32)
    not_last = (t < n_tiles - 1).astype(F32)
    h_scr[0:SUBLANE, :] = normed(xp_ref[...]) * not_first
    h_scr[SUBLANE:SUBLANE + tm, :] = normed(x)
    h_scr[SUBLANE + tm:, :] = normed(xn_ref[...]) * not_last
    h = h_scr[...].astype(BF16)
    fc = FFN_CHUNK

    def conv_cols(slot, col0):
        z_scr[slot] = jnp.dot(h, wup_ref[:, col0:col0 + fc], preferred_element_type=F32)
        cw = cw_ref[:, col0:col0 + fc]
        prev = z_scr[slot, SUBLANE - 1:SUBLANE - 1 + tm, :]
        cur = z_scr[slot, SUBLANE:SUBLANE + tm, :]
        nxt = z_scr[slot, SUBLANE + 1:SUBLANE + 1 + tm, :]
        return prev * cw[0:1, :] + cur * cw[1:2, :] + nxt * cw[2:3, :] + cb_ref[:, col0:col0 + fc]

    for f in range(FFN_HIDDEN // fc):
        val = conv_cols(0, f * fc)
        gate = conv_cols(1, FFN_HIDDEN + f * fc)
        act_scr[:, f * fc:(f + 1) * fc] = (gate * (1.0 / (1.0 + jnp.exp(-gate))) * val).astype(BF16)
    down = jnp.dot(act_scr[...], wdn_ref[...], preferred_element_type=F32)
    o_ref[...] = x + g2_ref[...] * down


def _ffn(x, sc2, sh2, g2, n2, wup, cw, cb, wdn):
    bsz, seq, d = x.shape
    tm = FFN_TILE
    n_tiles = seq // tm
    rows8 = tm // SUBLANE
    tok = pl.BlockSpec((None, tm, d), lambda b, t: (b, t, 0))
    prev8 = pl.BlockSpec((None, SUBLANE, d), lambda b, t: (b, jnp.maximum(t * rows8 - 1, 0), 0))
    next8 = pl.BlockSpec((None, SUBLANE, d),
                         lambda b, t: (b, jnp.minimum((t + 1) * rows8, seq // SUBLANE - 1), 0))
    per_b = pl.BlockSpec((None, 1, d), lambda b, t: (b, 0, 0))
    return pl.pallas_call(
        functools.partial(_ffn_kernel, n_tiles=n_tiles),
        grid=(bsz, n_tiles),
        in_specs=[tok, prev8, next8, per_b, per_b, per_b, _resident(n2.shape), _resident(wup.shape),
                  _resident(cw.shape), _resident(cb.shape), _resident(wdn.shape)],
        out_specs=tok,
        out_shape=jax.ShapeDtypeStruct((bsz, seq, d), F32),
        scratch_shapes=[pltpu.VMEM((tm + 2 * SUBLANE, d), F32),
                        pltpu.VMEM((2, tm + 2 * SUBLANE, FFN_CHUNK), F32),
                        pltpu.VMEM((tm, FFN_HIDDEN), BF16)],
        compiler_params=_cparams("parallel", "parallel"),
        name="ffn",
    )(x, x, x, sc2, sh2, g2, n2, wup, cw, cb, wdn)


def _swap_halves(w, heads, dim):
    half = dim // 2
    parts = []
    for hd in range(heads):
        parts += [w[..., hd * dim + half:(hd + 1) * dim], w[..., hd * dim:hd * dim + half]]
    return jnp.concatenate(parts, axis=-1)


def _prep_inproj(w_in, q_gain, w_uq, kv_gain, w_ukv, mla_gain, dil_gain):
    d = w_in.shape[0]
    z = lambda n: jnp.zeros((d, n), F32)
    a, b = w_in[:, :IN_A], w_in[:, IN_A:IN_A + IN_B]
    rope = a[:, MLA_Q_RANK + MLA_KV_RANK:]
    dq, dk, dv = b[:, :WIDTH_B], b[:, WIDTH_B:2 * WIDTH_B], b[:, 2 * WIDTH_B:]
    win = jnp.concatenate([
        a[:, :MLA_Q_RANK], z(Q_RANK_PAD - MLA_Q_RANK),
        a[:, MLA_Q_RANK:MLA_Q_RANK + MLA_KV_RANK],
        z(MLA_NOPE), rope, z(LANE - MLA_QK),
        z(MLA_NOPE), _swap_halves(rope, 1, MLA_ROPE), z(LANE - MLA_QK),
        dq, _swap_halves(dq, DIL_HEADS, DIL_HEAD_DIM),
        dk, _swap_halves(dk, DIL_HEADS, DIL_HEAD_DIM),
        dv, w_in[:, IN_A + IN_B:]], axis=1).astype(BF16)

    wq = w_uq.reshape(MLA_Q_RANK, MLA_HEADS, MLA_QK)
    zq = lambda n: jnp.zeros((MLA_Q_RANK, MLA_HEADS, n), F32)
    slab = jnp.concatenate([wq, zq(LANE - MLA_QK)], -1)
    slab_s = jnp.concatenate([zq(MLA_NOPE), wq[..., MLA_NOPE + MLA_ROPE // 2:],
                              wq[..., MLA_NOPE:MLA_NOPE + MLA_ROPE // 2], zq(LANE - MLA_QK)], -1)
    wuq = jnp.concatenate([slab.reshape(MLA_Q_RANK, QA_W), slab_s.reshape(MLA_Q_RANK, QA_W)], 1)
    wuq = jnp.pad(wuq, ((0, Q_RANK_PAD - MLA_Q_RANK), (0, 0))).astype(BF16)
    qg = jnp.pad(q_gain, (0, Q_RANK_PAD - MLA_Q_RANK)).reshape(1, Q_RANK_PAD)

    wkv = w_ukv.reshape(MLA_KV_RANK, MLA_HEADS, MLA_NOPE + MLA_V)
    k_slab = jnp.concatenate([wkv[..., :MLA_NOPE], jnp.zeros((MLA_KV_RANK, MLA_HEADS, LANE - MLA_NOPE), F32)], -1)
    wukv = jnp.concatenate([k_slab.reshape(MLA_KV_RANK, QA_W),
                            wkv[..., MLA_NOPE:].reshape(MLA_KV_RANK, WIDTH_A)], 1).astype(BF16)
    kvg = kv_gain.reshape(1, MLA_KV_RANK)

    def mla_rows(g):
        pad = jnp.zeros((LANE - MLA_QK,), F32)
        plain = jnp.concatenate([g, pad])
        swapped = jnp.concatenate([jnp.zeros((MLA_NOPE,), F32), g[MLA_NOPE + MLA_ROPE // 2:],
                                   g[MLA_NOPE:MLA_NOPE + MLA_ROPE // 2], pad])
        return [plain, swapped]

    def dil_rows(g):
        return [jnp.tile(g, 2), jnp.tile(_swap_halves(g, 1, DIL_HEAD_DIM), 2)]

    fill = [jnp.zeros((LANE,), F32)] * (SUBLANE - 4)
    a_gains = jnp.stack(mla_rows(mla_gain[0]) + mla_rows(mla_gain[1]) + fill)
    b_gains = jnp.stack(dil_rows(dil_gain[0]) + dil_rows(dil_gain[1]) + fill)
    return win, qg, wuq, kvg, wukv, a_gains, b_gains


def _rope_tables(positions):
    def tables(dim):
        inv_freq = 1.0 / (ROPE_THETA ** (jnp.arange(0, dim, 2, dtype=F32) / dim))
        ang = positions.astype(F32)[..., None] * inv_freq
        return jnp.cos(ang), jnp.sin(ang)

    cos_r, sin_r, cos_f, sin_f = lax.optimization_barrier(tables(MLA_ROPE) + tables(DIL_HEAD_DIM))
    lead = positions.shape
    ones = jnp.ones(lead + (MLA_NOPE,), F32)
    zeros = jnp.zeros(lead + (MLA_NOPE,), F32)
    pad = jnp.zeros(lead + (LANE - MLA_QK,), F32)
    ca = jnp.concatenate([ones, cos_r, cos_r, pad], -1)
    sa = jnp.concatenate([zeros, -sin_r, sin_r, pad], -1)
    cb = jnp.concatenate([cos_f, cos_f, cos_f, cos_f], -1)
    sb = jnp.concatenate([-sin_f, sin_f, -sin_f, sin_f], -1)
    return ca, sa, cb, sb


def kernel(x, c, positions, w_mod, b_mod, norm1, w_in, mla_q_norm, mla_w_uq, mla_kv_norm, mla_w_ukv, mla_qk_gain, dil_qk_gain, ssm_a_re, ssm_a_im, ssm_log_dt, ssm_b_re, ssm_b_im, ssm_c_re, ssm_c_im, ssm_d, ssm_w_glu, ssm_b_glu, mix_norm, w_out, norm2, ffn_w_up, ffn_conv_w, ffn_conv_b, ffn_w_down):
    bsz, seq, d = x.shape
    depth = w_mod.shape[0]
    assert d == D_MODEL and seq % TOKEN_TILE == 0 and seq % SSM_CHUNK == 0
    tabs = _rope_tables(positions)
    mod = _modulation(c, w_mod, b_mod)
    for l in range(depth):
        sh1, sc1, g1, sh2, sc2, g2 = (m.reshape(bsz, 1, d) for m in jnp.split(mod[l], N_MOD, axis=-1))
        win, qg, wuq, kvg, wukv, a_gains, b_gains = _prep_inproj(
            w_in[l], mla_q_norm[l], mla_w_uq[l], mla_kv_norm[l], mla_w_ukv[l], mla_qk_gain[l], dil_qk_gain[l])
        qa, ka, va, qb, kb, vb, *uc = _inproj(x, sc1, sh1, norm1[l].reshape(1, d), win, qg, wuq, kvg, wukv,
                                             a_gains, b_gains, tabs)
        oa = _attention(qa, ka, va)
        ob = _dilated_attention(qb, kb, vb)
        wz, mm, vt, at = _s5_matrices(ssm_a_re[l], ssm_a_im[l], ssm_log_dt[l], ssm_b_re[l], ssm_b_im[l],
                                      ssm_c_re[l], ssm_c_im[l], ssm_d[l])
        y = _s5_unpack(_s5(_s5_pack(uc), wz, mm, vt, at, bsz), bsz)
        x = _mix(oa, ob, y, x, g1, ssm_w_glu[l].astype(BF16), ssm_b_glu[l].reshape(1, -1),
                 mix_norm[l].reshape(1, -1), w_out[l].astype(BF16))
        x = _ffn(x, sc2, sh2, g2, norm2[l].reshape(1, d), ffn_w_up[l].astype(BF16), ffn_conv_w[l],
                 ffn_conv_b[l].reshape(1, -1), ffn_w_down[l].astype(BF16))
    return x
```

```python
import functools
import math

import jax
import jax.numpy as jnp
from jax import lax
from jax.experimental import pallas as pl
from jax.experimental.pallas import tpu as pltpu

F32 = jnp.float32
BF16 = jnp.bfloat16

D_MODEL = 1024
MLA_HEADS = 6
MLA_NOPE = 64
MLA_ROPE = 32
MLA_V = 64
MLA_QK = MLA_NOPE + MLA_ROPE
MLA_Q_RANK = 192
MLA_KV_RANK = 128
DIL_HEADS = 6
DIL_HEAD_DIM = 64
DIL_PATTERNS = ((128, 1), (512, 4), (2048, 16))
SSM_GROUPS = 16
SSM_GROUP_CH = 16
SSM_CH = SSM_GROUPS * SSM_GROUP_CH
SSM_STATE = 64
WIDTH_A = MLA_HEADS * MLA_V
WIDTH_B = DIL_HEADS * DIL_HEAD_DIM
MIX_WIDTH = WIDTH_A + WIDTH_B + SSM_CH
IN_A = MLA_Q_RANK + MLA_KV_RANK + MLA_ROPE
IN_B = 3 * WIDTH_B
FFN_HIDDEN = 2816
ROPE_THETA = 10000.0
EPS = 1e-6
NEG_INF = -1e30
N_MOD = 6

LANE = 128
SUBLANE = 8
HALF = LANE // 2
Q_RANK_PAD = 256
TOKEN_TILE = 512
FFN_TILE = 1024
ATTN_Q_TILE = 1024
ATTN_PV_SPLIT = 2
DIL_Q_BLOCK = 128
DIL_GROUP = 4
FFN_CHUNK = 256
SSM_CHUNK = 32
VMEM_LIMIT = 56 * 1024 * 1024

C_CQ = 0
C_CKV = C_CQ + Q_RANK_PAD
C_KR = C_CKV + MLA_KV_RANK
C_KRS = C_KR + LANE
C_DQ = C_KRS + LANE
C_DQS = C_DQ + WIDTH_B
C_DK = C_DQS + WIDTH_B
C_DKS = C_DK + WIDTH_B
C_DV = C_DKS + WIDTH_B
C_UC = C_DV + WIDTH_B
IN_EXT = C_UC + SSM_CH
N_PAIRS = MLA_HEADS // 2
QA_W = MLA_HEADS * LANE


def _cparams(*sem):
    return pltpu.CompilerParams(dimension_semantics=sem, vmem_limit_bytes=VMEM_LIMIT)


def _resident(shape):
    nd = len(shape)
    return pl.BlockSpec(shape, lambda *_: (0,) * nd, pipeline_mode=pl.Buffered(1))


def _rms_scale(v, width):
    return lax.rsqrt(jnp.sum(v * v, axis=-1, keepdims=True) * (1.0 / width) + EPS)


def _bdot(a, b):
    return jnp.dot(a.astype(BF16), b, preferred_element_type=F32)


def _mod_kernel(c_ref, w_ref, b_ref, o_ref):
    c = c_ref[...]
    ca = c * (1.0 / (1.0 + jnp.exp(-c)))
    w = w_ref[...]
    c_hi = ca.astype(BF16)
    c_lo = (ca - c_hi.astype(F32)).astype(BF16)
    w_hi = w.astype(BF16)
    w_lo = (w - w_hi.astype(F32)).astype(BF16)
    acc = jnp.dot(c_hi, w_hi, preferred_element_type=F32)
    acc += jnp.dot(c_lo, w_hi, preferred_element_type=F32)
    acc += jnp.dot(c_hi, w_lo, preferred_element_type=F32)
    o_ref[...] = acc + b_ref[...]


def _modulation(c, w_mod, b_mod):
    depth, d, n = w_mod.shape
    bsz = c.shape[0]
    tn = 1024
    return pl.pallas_call(
        _mod_kernel,
        grid=(depth, n // tn),
        in_specs=[
            pl.BlockSpec((bsz, d), lambda l, j: (0, 0)),
            pl.BlockSpec((None, d, tn), lambda l, j: (l, 0, j)),
            pl.BlockSpec((None, 1, tn), lambda l, j: (l, 0, j)),
        ],
        out_specs=pl.BlockSpec((None, bsz, tn), lambda l, j: (l, 0, j)),
        out_shape=jax.ShapeDtypeStruct((depth, bsz, n), F32),
        compiler_params=_cparams("parallel", "parallel"),
        name="modulation",
    )(c, w_mod, b_mod.reshape(depth, 1, n))


def _inproj_kernel(x_ref, sc_ref, sh_ref, n1_ref, win_ref, qg_ref, wuq_ref, kvg_ref, wukv_ref,
                   ag_ref, bg_ref, ca_ref, sa_ref, cb_ref, sb_ref,
                   qa_ref, ka_ref, va_ref, qb_ref, kb_ref, vb_ref, *uc_refs):
    x = x_ref[...]
    h = x * _rms_scale(x, D_MODEL) * n1_ref[...]
    h = h * (1.0 + sc_ref[...]) + sh_ref[...]
    u = _bdot(h, win_ref[...])

    cq = u[:, C_CQ:C_CQ + Q_RANK_PAD]
    cqn = cq * _rms_scale(cq, MLA_Q_RANK) * qg_ref[...]
    q_all = _bdot(cqn, wuq_ref[...])
    ckv = u[:, C_CKV:C_CKV + MLA_KV_RANK]
    ckvn = ckv * _rms_scale(ckv, MLA_KV_RANK) * kvg_ref[...]
    kv_all = _bdot(ckvn, wukv_ref[...])
    kr = u[:, C_KR:C_KR + LANE]
    krs = u[:, C_KRS:C_KRS + LANE]

    ca, sa = ca_ref[...], sa_ref[...]
    gq, gqs, gk, gks = ag_ref[0:1, :], ag_ref[1:2, :], ag_ref[2:3, :], ag_ref[3:4, :]
    cos_q, sin_q = ca * gq, sa * gqs
    cos_k, sin_k = ca * gk, sa * gks
    krs_rot = krs * sin_k
    a_scale = MLA_QK ** -0.5
    for hd in range(MLA_HEADS):
        q = q_all[:, hd * LANE:(hd + 1) * LANE]
        qs = q_all[:, QA_W + hd * LANE:QA_W + (hd + 1) * LANE]
        r = _rms_scale(q, MLA_QK) * a_scale
        qa_ref[:, hd * LANE:(hd + 1) * LANE] = ((q * cos_q + qs * sin_q) * r).astype(BF16)
        k = kv_all[:, hd * LANE:(hd + 1) * LANE] + kr
        r = _rms_scale(k, MLA_QK)
        ka_ref[:, hd * LANE:(hd + 1) * LANE] = ((k * cos_k + krs_rot) * r).astype(BF16)
    va_ref[...] = kv_all[:, QA_W:QA_W + WIDTH_A].astype(BF16)

    cb, sb = cb_ref[...], sb_ref[...]
    gq, gqs, gk, gks = bg_ref[0:1, :], bg_ref[1:2, :], bg_ref[2:3, :], bg_ref[3:4, :]
    cos_q, sin_q = cb * gq, sb * gqs
    cos_k, sin_k = cb * gk, sb * gks
    low = lax.broadcasted_iota(jnp.int32, (x.shape[0], LANE), 1) < HALF
    b_scale = DIL_HEAD_DIM ** -0.5

    def pair_scale(v):
        sq = v * v
        s_all = jnp.sum(sq, axis=-1, keepdims=True)
        s_low = jnp.sum(jnp.where(low, sq, 0.0), axis=-1, keepdims=True)
        r_low = lax.rsqrt(s_low * (1.0 / DIL_HEAD_DIM) + EPS)
        r_high = lax.rsqrt((s_all - s_low) * (1.0 / DIL_HEAD_DIM) + EPS)
        return jnp.where(low, r_low, r_high)

    for p in range(N_PAIRS):
        q = u[:, C_DQ + p * LANE:C_DQ + (p + 1) * LANE]
        qs = u[:, C_DQS + p * LANE:C_DQS + (p + 1) * LANE]
        qo = (q * cos_q + qs * sin_q) * (pair_scale(q) * b_scale)
        qb_ref[:, 2 * p * LANE:(2 * p + 1) * LANE] = jnp.where(low, qo, 0.0).astype(BF16)
        qb_ref[:, (2 * p + 1) * LANE:(2 * p + 2) * LANE] = jnp.where(low, 0.0, qo).astype(BF16)
        k = u[:, C_DK + p * LANE:C_DK + (p + 1) * LANE]
        ks = u[:, C_DKS + p * LANE:C_DKS + (p + 1) * LANE]
        kb_ref[:, p * LANE:(p + 1) * LANE] = ((k * cos_k + ks * sin_k) * pair_scale(k)).astype(BF16)
    vb_ref[...] = u[:, C_DV:C_DV + WIDTH_B].astype(BF16)
    for i, uc_ref in enumerate(uc_refs):
        uc_ref[...] = u[:, C_UC + i * LANE:C_UC + (i + 1) * LANE]


def _inproj(x, sc1, sh1, n1, win, qg, wuq, kvg, wukv, a_gains, b_gains, tabs):
    bsz, seq, d = x.shape
    tm = TOKEN_TILE
    tok = lambda w: pl.BlockSpec((None, tm, w), lambda b, t: (b, t, 0))
    per_b = pl.BlockSpec((None, 1, d), lambda b, t: (b, 0, 0))
    outs = [(QA_W, BF16), (QA_W, BF16), (WIDTH_A, BF16), (2 * WIDTH_B, BF16), (WIDTH_B, BF16),
            (WIDTH_B, BF16)] + [(LANE, F32)] * (SSM_CH // LANE)
    return pl.pallas_call(
        _inproj_kernel,
        grid=(bsz, seq // tm),
        in_specs=[tok(d), per_b, per_b, _resident((1, d)), _resident(win.shape), _resident(qg.shape),
                  _resident(wuq.shape), _resident(kvg.shape), _resident(wukv.shape),
                  _resident(a_gains.shape), _resident(b_gains.shape),
                  tok(LANE), tok(LANE), tok(LANE), tok(LANE)],
        out_specs=[tok(w) for w, _ in outs],
        out_shape=[jax.ShapeDtypeStruct((bsz, seq, w), dt) for w, dt in outs],
        compiler_params=_cparams("parallel", "parallel"),
        name="inproj",
    )(x, sc1, sh1, n1, win, qg, wuq, kvg, wukv, a_gains, b_gains, *tabs)


def _attn_kernel(q_ref, k_ref, v_ref, o_ref):
    v = v_ref[...]
    low_k = lax.broadcasted_iota(jnp.int32, v.shape, 1) < HALF
    outs = []
    for j in range(2):
        q = q_ref[:, j * LANE:(j + 1) * LANE]
        k = k_ref[:, j * LANE:(j + 1) * LANE]
        s = lax.dot_general(q, k, (((1,), (1,)), ((), ())), preferred_element_type=F32)
        m = jnp.max(s, axis=-1, keepdims=True)
        pb = jnp.exp(s - m).astype(BF16)
        vb = jnp.where(low_k == (j == 0), v, jnp.ones_like(v))
        rows = pb.shape[0] // ATTN_PV_SPLIT
        o = jnp.concatenate([jnp.dot(pb[i * rows:(i + 1) * rows], vb, preferred_element_type=F32)
                             for i in range(ATTN_PV_SPLIT)], axis=0)
        outs.append(o * (1.0 / pltpu.roll(o, HALF, axis=1)))
    low = lax.broadcasted_iota(jnp.int32, outs[0].shape, 1) < HALF
    o_ref[...] = jnp.where(low, outs[0], outs[1]).astype(o_ref.dtype)


def _attention(q, k, v):
    bsz, seq, _ = q.shape
    tq = ATTN_Q_TILE
    return pl.pallas_call(
        _attn_kernel,
        grid=(bsz, N_PAIRS, seq // tq),
        in_specs=[
            pl.BlockSpec((None, tq, 2 * LANE), lambda b, p, i: (b, i, p)),
            pl.BlockSpec((None, seq, 2 * LANE), lambda b, p, i: (b, 0, p)),
            pl.BlockSpec((None, seq, LANE), lambda b, p, i: (b, 0, p)),
        ],
        out_specs=pl.BlockSpec((None, tq, LANE), lambda b, p, i: (b, i, p)),
        out_shape=jax.ShapeDtypeStruct((bsz, seq, N_PAIRS * LANE), BF16),
        compiler_params=_cparams("parallel", "parallel", "parallel"),
        name="mla_attention",
    )(q, k, v)


def _dilated_kernel(q_ref, k_ref, v_ref, o_ref, qf, kf, vf, acc, mx):
    seq = k_ref.shape[0]
    kf[...] = k_ref[...].astype(F32)
    vf[...] = v_ref[...].astype(F32)
    for j in range(2):
        qf[j] = q_ref[:, j * LANE:(j + 1) * LANE].astype(F32)
    qb = DIL_Q_BLOCK
    for branch, (window, dil) in enumerate(DIL_PATTERNS):
        half = window // (2 * dil)
        n_sub = seq // dil
        kw = min(n_sub, qb + 2 * half)
        key_minus_query = (lax.broadcasted_iota(jnp.int32, (qb, kw), 1)
                           - lax.broadcasted_iota(jnp.int32, (qb, kw), 0))
        low_k = lax.broadcasted_iota(jnp.int32, (kw, LANE), 1) < HALF
        blocks = [(res, j0) for res in range(dil) for j0 in range(0, n_sub, qb)]
        for g0 in range(0, len(blocks), DIL_GROUP):
            chains = []
            for res, j0 in blocks[g0:g0 + DIL_GROUP]:
                k0 = min(max(j0 - half, 0), n_sub - kw)
                band = jnp.where(jnp.abs(key_minus_query + (k0 - j0)) <= half, 0.0, NEG_INF)
                q_rows = pl.ds(res + dil * j0, qb, stride=dil)
                k_rows = pl.ds(res + dil * k0, kw, stride=dil)
                kb = kf[k_rows, :].astype(BF16)
                v_win = vf[k_rows, :]
                for j in range(2):
                    vb = jnp.where(low_k == (j == 0), v_win, 1.0).astype(BF16)
                    q = qf[j, q_rows, :].astype(BF16)
                    s = lax.dot_general(q, kb, (((1,), (1,)), ((), ())), preferred_element_type=F32) + band
                    chains.append((j, q_rows, s, vb))
            probs = []
            for j, q_rows, s, vb in chains:
                m = jnp.max(s, axis=-1, keepdims=True)
                mx[branch, j, q_rows, :] = jnp.broadcast_to(m, (qb, LANE))
                probs.append(jnp.exp(s - m).astype(BF16))
            for (j, q_rows, s, vb), p in zip(chains, probs):
                acc[branch, j, q_rows, :] = jnp.dot(p, vb, preferred_element_type=F32)
    outs = []
    n_br = len(DIL_PATTERNS)
    for j in range(2):
        m_all = mx[0, j]
        for br in range(1, n_br):
            m_all = jnp.maximum(m_all, mx[br, j])
        tot = None
        for br in range(n_br):
            w = jnp.exp(mx[br, j] - m_all)
            tot = w * acc[br, j] if tot is None else tot + w * acc[br, j]
        outs.append(tot * (1.0 / pltpu.roll(tot, HALF, axis=1)))
    low = lax.broadcasted_iota(jnp.int32, (seq, LANE), 1) < HALF
    o_ref[...] = jnp.where(low, outs[0], outs[1]).astype(o_ref.dtype)


def _dilated_attention(q, k, v):
    bsz, seq, _ = q.shape
    assert all(seq % (dil * DIL_Q_BLOCK) == 0 for _, dil in DIL_PATTERNS)
    return pl.pallas_call(
        _dilated_kernel,
        grid=(bsz, N_PAIRS),
        in_specs=[
            pl.BlockSpec((None, seq, 2 * LANE), lambda b, p: (b, 0, p)),
            pl.BlockSpec((None, seq, LANE), lambda b, p: (b, 0, p)),
            pl.BlockSpec((None, seq, LANE), lambda b, p: (b, 0, p)),
        ],
        out_specs=pl.BlockSpec((None, seq, LANE), lambda b, p: (b, 0, p)),
        out_shape=jax.ShapeDtypeStruct((bsz, seq, N_PAIRS * LANE), BF16),
        scratch_shapes=[pltpu.VMEM((2, seq, LANE), F32), pltpu.VMEM((seq, LANE), F32),
                        pltpu.VMEM((seq, LANE), F32)]
        + [pltpu.VMEM((len(DIL_PATTERNS), 2, seq, LANE), F32)] * 2,
        compiler_params=_cparams("parallel", "parallel"),
        name="dilated_attention",
    )(q, k, v)


def _s5_kernel(u_ref, wz_ref, m_ref, vt_ref, at_ref, y_ref, z_scr, s_scr, *, n_chunks, bsz):
    u = u_ref[...]
    z = jnp.dot(u, wz_ref[...], preferred_element_type=F32)
    for slab in range(4):
        z_scr[slab] = z[:, slab * LANE:(slab + 1) * LANE]
    at = at_ref[...]
    ar_f, ai_f, ar_b, ai_b = at[0:1, :], at[1:2, :], at[2:3, :], at[3:4, :]

    def chunk_rows(kk):
        return pl.ds(kk, bsz, stride=n_chunks)

    def step(i, carry):
        re_f, im_f, re_b, im_b = carry
        kf, kb = chunk_rows(i), chunk_rows(n_chunks - 1 - i)
        s_scr[0, kf, :] = re_f
        s_scr[1, kf, :] = im_f
        s_scr[2, kb, :] = re_b
        s_scr[3, kb, :] = im_b
        return (ar_f * re_f - ai_f * im_f + z_scr[0, kf, :], ar_f * im_f + ai_f * re_f + z_scr[1, kf, :],
                ar_b * re_b - ai_b * im_b + z_scr[2, kb, :], ar_b * im_b + ai_b * re_b + z_scr[3, kb, :])

    zero = jnp.zeros((bsz, LANE), F32)
    lax.fori_loop(0, n_chunks, step, (zero, zero, zero, zero), unroll=2)
    y = jnp.dot(u, m_ref[...], preferred_element_type=F32)
    states = jnp.concatenate([s_scr[slab] for slab in range(4)], axis=1).astype(BF16)
    y += lax.dot_general(states, vt_ref[...], (((1,), (1,)), ((), ())), preferred_element_type=F32)
    y_ref[...] = y


def _s5(u_g, wz, mm, vt, at, bsz):
    groups, rows, width = u_g.shape
    n_chunks = rows // bsz
    sw = 4 * LANE
    return pl.pallas_call(
        functools.partial(_s5_kernel, n_chunks=n_chunks, bsz=bsz),
        grid=(groups,),
        in_specs=[
            pl.BlockSpec((None, rows, width), lambda g: (g, 0, 0)),
            pl.BlockSpec((None, width, sw), lambda g: (g, 0, 0)),
            pl.BlockSpec((None, width, width), lambda g: (g, 0, 0)),
            pl.BlockSpec((None, width, sw), lambda g: (g, 0, 0)),
            pl.BlockSpec((None, SUBLANE, LANE), lambda g: (g, 0, 0)),
        ],
        out_specs=pl.BlockSpec((None, rows, width), lambda g: (g, 0, 0)),
        out_shape=jax.ShapeDtypeStruct((groups, rows, width), F32),
        scratch_shapes=[pltpu.VMEM((4, rows, LANE), F32), pltpu.VMEM((4, rows, LANE), F32)],
        compiler_params=_cparams("parallel"),
        name="s5",
    )(u_g, wz, mm, vt, at)


PIECE = SSM_GROUP_CH
PIECES = LANE // PIECE
GROUP_TILES = SSM_CH // LANE


def _piece_gather(load_tile, shifts_and_tiles):
    acc = None
    for slot, (tile_idx, shift) in enumerate(shifts_and_tiles):
        src = load_tile(tile_idx)
        if shift % LANE:
            src = pltpu.roll(src, shift % LANE, axis=1)
        if acc is None:
            acc = src
        else:
            piece = lax.broadcasted_iota(jnp.int32, src.shape, 1) // PIECE
            acc = jnp.where(piece == slot, src, acc)
    return acc


def _s5_pack_kernel(*refs):
    z_refs, o_ref = refs[:GROUP_TILES], refs[GROUP_TILES]
    n_chunks = o_ref.shape[1]

    tiles = [z_refs[i % GROUP_TILES][pl.ds(i // GROUP_TILES, n_chunks, stride=SSM_CHUNK), :]
             for i in range(SSM_CHUNK * GROUP_TILES)]
    for g in range(SSM_GROUPS):
        q = g % PIECES
        for jt in range(SSM_CHUNK // PIECES):
            plan = [((jt * PIECES + jj) * GROUP_TILES + g // PIECES, (jj - q) * PIECE) for jj in range(PIECES)]
            o_ref[g, :, jt * LANE:(jt + 1) * LANE] = _piece_gather(lambda i: tiles[i], plan).astype(BF16)


def _s5_unpack_kernel(y_ref, o_ref):
    for j in range(SSM_CHUNK):
        jj = j % PIECES
        jt = j // PIECES
        for half in range(GROUP_TILES):
            plan = [((half * PIECES + q), (q - jj) * PIECE) for q in range(PIECES)]
            tile = _piece_gather(lambda g: y_ref[g, :, jt * LANE:(jt + 1) * LANE], plan)
            o_ref[:, (j * GROUP_TILES + half) * LANE:(j * GROUP_TILES + half + 1) * LANE] = tile.astype(BF16)


def _s5_pack(uc_halves):
    bsz, seq, _ = uc_halves[0].shape
    n_chunks = seq // SSM_CHUNK
    width = SSM_CHUNK * SSM_GROUP_CH
    return pl.pallas_call(
        _s5_pack_kernel,
        grid=(bsz,),
        in_specs=[pl.BlockSpec((None, seq, LANE), lambda b: (b, 0, 0))] * GROUP_TILES,
        out_specs=pl.BlockSpec((SSM_GROUPS, n_chunks, width), lambda b: (0, b, 0)),
        out_shape=jax.ShapeDtypeStruct((SSM_GROUPS, bsz * n_chunks, width), BF16),
        compiler_params=_cparams("parallel"),
        name="s5_pack",
    )(*uc_halves)


def _s5_unpack(y_g, bsz):
    groups, rows, width = y_g.shape
    n_chunks = rows // bsz
    out = pl.pallas_call(
        _s5_unpack_kernel,
        grid=(bsz,),
        in_specs=[pl.BlockSpec((groups, n_chunks, width), lambda b: (0, b, 0))],
        out_specs=pl.BlockSpec((None, n_chunks, SSM_CHUNK * SSM_CH), lambda b: (b, 0, 0)),
        out_shape=jax.ShapeDtypeStruct((bsz, n_chunks, SSM_CHUNK * SSM_CH), BF16),
        compiler_params=_cparams("parallel"),
        name="s5_unpack",
    )(y_g)
    return out.reshape(bsz, n_chunks * SSM_CHUNK, SSM_CH)


def _s5_matrices(a_re, a_im, log_dt, b_re, b_im, c_re, c_im, d_skip):
    t = SSM_CHUNK
    g, p, cg = SSM_GROUPS, SSM_STATE, SSM_GROUP_CH
    a = lax.complex(a_re.astype(F32), a_im.astype(F32))
    dt = jnp.exp(log_dt.astype(F32))[..., None]
    lam = a * dt
    a_bar = jnp.exp(lam)
    b_bar = ((a_bar - 1.0) / a)[..., None] * lax.complex(b_re.astype(F32), b_im.astype(F32))
    c = lax.complex(c_re.astype(F32), c_im.astype(F32))
    n = jnp.arange(t + 1, dtype=F32)
    pw = jnp.exp(lam[:, None] * n[None, :, None, None].astype(jnp.complex64))
    kern = jnp.einsum("dgcp,dtgp,dgpe->dtgce", c, pw[:, :t], b_bar).real
    eye = jnp.eye(cg, dtype=F32) * d_skip.astype(F32).reshape(g, cg)[:, :, None]
    lags = jnp.concatenate([kern[1][1:][::-1], (kern[0][0] + kern[1][0] + eye)[None], kern[0][1:]], 0)
    strip = lags.transpose(1, 3, 0, 2).reshape(g, cg, (2 * t - 1) * cg)
    mm = jnp.stack([strip[:, :, (t - 1 - j) * cg:(2 * t - 1 - j) * cg] for j in range(t)], axis=1)
    mm = mm.reshape(g, t * cg, t * cg)
    def lane_padded(v):
        return jnp.pad(v, [(0, 0)] * (v.ndim - 1) + [(0, LANE - p)])

    pw_g = pw.transpose(0, 2, 1, 3)
    pw_re, pw_im = lane_padded(pw_g.real), lane_padded(pw_g.imag)
    b_t = b_bar.transpose(0, 1, 3, 2)
    bt_re, bt_im = lane_padded(b_t.real), lane_padded(b_t.imag)
    c_re_p, c_im_p = lane_padded(c.real), lane_padded(c.imag)

    def slabs(a_re_, a_im_, m_re, m_im, sign):
        a_re_, a_im_ = a_re_[:, :, None, :], a_im_[:, :, None, :]
        m_re, m_im = m_re[:, None], m_im[:, None]
        re = a_re_ * m_re - a_im_ * m_im
        im = a_re_ * m_im + a_im_ * m_re
        return [re.reshape(g, t * cg, LANE), (sign * im).reshape(g, t * cg, LANE)]

    wz = jnp.concatenate(
        slabs(pw_re[0, :, :t][:, ::-1], pw_im[0, :, :t][:, ::-1], bt_re[0], bt_im[0], 1.0)
        + slabs(pw_re[1, :, :t], pw_im[1, :, :t], bt_re[1], bt_im[1], 1.0), axis=-1)
    vt = jnp.concatenate(
        slabs(pw_re[0, :, 1:t + 1], pw_im[0, :, 1:t + 1], c_re_p[0], c_im_p[0], -1.0)
        + slabs(pw_re[1, :, 1:t + 1][:, ::-1], pw_im[1, :, 1:t + 1][:, ::-1], c_re_p[1], c_im_p[1], -1.0),
        axis=-1)
    at_c = pw[:, t]
    lane_pad = jnp.zeros((g, LANE - p), F32)
    rows = [jnp.concatenate([at_c[0].real, lane_pad], -1), jnp.concatenate([at_c[0].imag, lane_pad], -1),
            jnp.concatenate([at_c[1].real, lane_pad], -1), jnp.concatenate([at_c[1].imag, lane_pad], -1)]
    at = jnp.stack(rows + [jnp.zeros((g, LANE), F32)] * (SUBLANE - 4), axis=1)
    return wz.astype(BF16), mm.astype(BF16), vt.astype(BF16), at


def _mix_kernel(oa_ref, ob_ref, y_ref, x_ref, g1_ref, wglu_ref, bglu_ref, gm_ref, wo_ref, o_ref):
    y = y_ref[...].astype(F32)
    y = 0.5 * y * (1.0 + jnp.tanh(math.sqrt(2.0 / math.pi) * (y + 0.044715 * (y * y * y))))
    z = _bdot(y, wglu_ref[...]) + bglu_ref[...]
    gate = z[:, SSM_CH:]
    oc = z[:, :SSM_CH] * (1.0 / (1.0 + jnp.exp(-gate)))
    oa, ob = oa_ref[...].astype(F32), ob_ref[...].astype(F32)
    na = oa * _rms_scale(oa, WIDTH_A) * gm_ref[:, 0:WIDTH_A]
    nb = ob * _rms_scale(ob, WIDTH_B) * gm_ref[:, WIDTH_A:WIDTH_A + WIDTH_B]
    nc = oc * _rms_scale(oc, SSM_CH) * gm_ref[:, WIDTH_A + WIDTH_B:]
    acc = _bdot(na, wo_ref[0:WIDTH_A, :])
    acc += _bdot(nb, wo_ref[WIDTH_A:WIDTH_A + WIDTH_B, :])
    acc += _bdot(nc, wo_ref[WIDTH_A + WIDTH_B:, :])
    o_ref[...] = x_ref[...] + g1_ref[...] * acc


def _mix(oa, ob, y, x, g1, wglu, bglu, gm, wo):
    bsz, seq, d = x.shape
    tm = TOKEN_TILE
    tok = lambda w: pl.BlockSpec((None, tm, w), lambda b, t: (b, t, 0))
    per_b = pl.BlockSpec((None, 1, d), lambda b, t: (b, 0, 0))
    return pl.pallas_call(
        _mix_kernel,
        grid=(bsz, seq // tm),
        in_specs=[tok(WIDTH_A), tok(WIDTH_B), tok(SSM_CH), tok(d), per_b, _resident(wglu.shape),
                  _resident(bglu.shape), _resident(gm.shape), _resident(wo.shape)],
        out_specs=tok(d),
        out_shape=jax.ShapeDtypeStruct((bsz, seq, d), F32),
        compiler_params=_cparams("parallel", "parallel"),
        name="mix",
    )(oa, ob, y, x, g1, wglu, bglu, gm, wo)


def _ffn_kernel(x_ref, xp_ref, xn_ref, sc_ref, sh_ref, g2_ref, n2_ref, wup_ref, cw_ref, cb_ref, wdn_ref,
                o_ref, h_scr, z_scr, act_scr, *, n_tiles):
    tm = x_ref.shape[0]
    t = pl.program_id(1)
    mod_scale = n2_ref[...] * (1.0 + sc_ref[...])
    shift = sh_ref[...]

    def normed(v):
        return v * _rms_scale(v, D_MODEL) * mod_scale + shift

    x = x_ref[...]
    not_first = (t > 0).astype(F32)
    not_last = (t < n_tiles - 1).astype(F32)
    h_scr[0:SUBLANE, :] = normed(xp_ref[...]) * not_first
    h_scr[SUBLANE:SUBLANE + tm, :] = normed(x)
    h_scr[SUBLANE + tm:, :] = normed(xn_ref[...]) * not_last
    h = h_scr[...].astype(BF16)
    fc = FFN_CHUNK

    def conv_cols(slot, col0):
        z_scr[slot] = jnp.dot(h, wup_ref[:, col0:col0 + fc], preferred_element_type=F32)
        cw = cw_ref[:, col0:col0 + fc]
        prev = z_scr[slot, SUBLANE - 1:SUBLANE - 1 + tm, :]
        cur = z_scr[slot, SUBLANE:SUBLANE + tm, :]
        nxt = z_scr[slot, SUBLANE + 1:SUBLANE + 1 + tm, :]
        return prev * cw[0:1, :] + cur * cw[1:2, :] + nxt * cw[2:3, :] + cb_ref[:, col0:col0 + fc]

    for f in range(FFN_HIDDEN // fc):
        val = conv_cols(0, f * fc)
        gate = conv_cols(1, FFN_HIDDEN + f * fc)
        act_scr[:, f * fc:(f + 1) * fc] = (gate * (1.0 / (1.0 + jnp.exp(-gate))) * val).astype(BF16)
    down = jnp.dot(act_scr[...], wdn_ref[...], preferred_element_type=F32)
    o_ref[...] = x + g2_ref[...] * down


def _ffn(x, sc2, sh2, g2, n2, wup, cw, cb, wdn):
    bsz, seq, d = x.shape
    tm = FFN_TILE
    n_tiles = seq // tm
    rows8 = tm // SUBLANE
    tok = pl.BlockSpec((None, tm, d), lambda b, t: (b, t, 0))
    prev8 = pl.BlockSpec((None, SUBLANE, d), lambda b, t: (b, jnp.maximum(t * rows8 - 1, 0), 0))
    next8 = pl.BlockSpec((None, SUBLANE, d),
                         lambda b, t: (b, jnp.minimum((t + 1) * rows8, seq // SUBLANE - 1), 0))
    per_b = pl.BlockSpec((None, 1, d), lambda b, t: (b, 0, 0))
    return pl.pallas_call(
        functools.partial(_ffn_kernel, n_tiles=n_tiles),
        grid=(bsz, n_tiles),
        in_specs=[tok, prev8, next8, per_b, per_b, per_b, _resident(n2.shape), _resident(wup.shape),
                  _resident(cw.shape), _resident(cb.shape), _resident(wdn.shape)],
        out_specs=tok,
        out_shape=jax.ShapeDtypeStruct((bsz, seq, d), F32),
        scratch_shapes=[pltpu.VMEM((tm + 2 * SUBLANE, d), F32),
                        pltpu.VMEM((2, tm + 2 * SUBLANE, FFN_CHUNK), F32),
                        pltpu.VMEM((tm, FFN_HIDDEN), BF16)],
        compiler_params=_cparams("parallel", "parallel"),
        name="ffn",
    )(x, x, x, sc2, sh2, g2, n2, wup, cw, cb, wdn)


def _swap_halves(w, heads, dim):
    half = dim // 2
    parts = []
    for hd in range(heads):
        parts += [w[..., hd * dim + half:(hd + 1) * dim], w[..., hd * dim:hd * dim + half]]
    return jnp.concatenate(parts, axis=-1)


def _prep_inproj(w_in, q_gain, w_uq, kv_gain, w_ukv, mla_gain, dil_gain):
    d = w_in.shape[0]
    z = lambda n: jnp.zeros((d, n), F32)
    a, b = w_in[:, :IN_A], w_in[:, IN_A:IN_A + IN_B]
    rope = a[:, MLA_Q_RANK + MLA_KV_RANK:]
    dq, dk, dv = b[:, :WIDTH_B], b[:, WIDTH_B:2 * WIDTH_B], b[:, 2 * WIDTH_B:]
    win = jnp.concatenate([
        a[:, :MLA_Q_RANK], z(Q_RANK_PAD - MLA_Q_RANK),
        a[:, MLA_Q_RANK:MLA_Q_RANK + MLA_KV_RANK],
        z(MLA_NOPE), rope, z(LANE - MLA_QK),
        z(MLA_NOPE), _swap_halves(rope, 1, MLA_ROPE), z(LANE - MLA_QK),
        dq, _swap_halves(dq, DIL_HEADS, DIL_HEAD_DIM),
        dk, _swap_halves(dk, DIL_HEADS, DIL_HEAD_DIM),
        dv, w_in[:, IN_A + IN_B:]], axis=1).astype(BF16)

    wq = w_uq.reshape(MLA_Q_RANK, MLA_HEADS, MLA_QK)
    zq = lambda n: jnp.zeros((MLA_Q_RANK, MLA_HEADS, n), F32)
    slab = jnp.concatenate([wq, zq(LANE - MLA_QK)], -1)
    slab_s = jnp.concatenate([zq(MLA_NOPE), wq[..., MLA_NOPE + MLA_ROPE // 2:],
                              wq[..., MLA_NOPE:MLA_NOPE + MLA_ROPE // 2], zq(LANE - MLA_QK)], -1)
    wuq = jnp.concatenate([slab.reshape(MLA_Q_RANK, QA_W), slab_s.reshape(MLA_Q_RANK, QA_W)], 1)
    wuq = jnp.pad(wuq, ((0, Q_RANK_PAD - MLA_Q_RANK), (0, 0))).astype(BF16)
    qg = jnp.pad(q_gain, (0, Q_RANK_PAD - MLA_Q_RANK)).reshape(1, Q_RANK_PAD)

    wkv = w_ukv.reshape(MLA_KV_RANK, MLA_HEADS, MLA_NOPE + MLA_V)
    k_slab = jnp.concatenate([wkv[..., :MLA_NOPE], jnp.zeros((MLA_KV_RANK, MLA_HEADS, LANE - MLA_NOPE), F32)], -1)
    wukv = jnp.concatenate([k_slab.reshape(MLA_KV_RANK, QA_W),
                            wkv[..., MLA_NOPE:].reshape(MLA_KV_RANK, WIDTH_A)], 1).astype(BF16)
    kvg = kv_gain.reshape(1, MLA_KV_RANK)

    def mla_rows(g):
        pad = jnp.zeros((LANE - MLA_QK,), F32)
        plain = jnp.concatenate([g, pad])
        swapped = jnp.concatenate([jnp.zeros((MLA_NOPE,), F32), g[MLA_NOPE + MLA_ROPE // 2:],
                                   g[MLA_NOPE:MLA_NOPE + MLA_ROPE // 2], pad])
        return [plain, swapped]

    def dil_rows(g):
        return [jnp.tile(g, 2), jnp.tile(_swap_halves(g, 1, DIL_HEAD_DIM), 2)]

    fill = [jnp.zeros((LANE,), F32)] * (SUBLANE - 4)
    a_gains = jnp.stack(mla_rows(mla_gain[0]) + mla_rows(mla_gain[1]) + fill)
    b_gains = jnp.stack(dil_rows(dil_gain[0]) + dil_rows(dil_gain[1]) + fill)
    return win, qg, wuq, kvg, wukv, a_gains, b_gains


def _rope_tables(positions):
    def tables(dim):
        inv_freq = 1.0 / (ROPE_THETA ** (jnp.arange(0, dim, 2, dtype=F32) / dim))
        ang = positions.astype(F32)[..., None] * inv_freq
        return jnp.cos(ang), jnp.sin(ang)

    cos_r, sin_r, cos_f, sin_f = lax.optimization_barrier(tables(MLA_ROPE) + tables(DIL_HEAD_DIM))
    lead = positions.shape
    ones = jnp.ones(lead + (MLA_NOPE,), F32)
    zeros = jnp.zeros(lead + (MLA_NOPE,), F32)
    pad = jnp.zeros(lead + (LANE - MLA_QK,), F32)
    ca = jnp.concatenate([ones, cos_r, cos_r, pad], -1)
    sa = jnp.concatenate([zeros, -sin_r, sin_r, pad], -1)
    cb = jnp.concatenate([cos_f, cos_f, cos_f, cos_f], -1)
    sb = jnp.concatenate([-sin_f, sin_f, -sin_f, sin_f], -1)
    return ca, sa, cb, sb


def kernel(x, c, positions, w_mod, b_mod, norm1, w_in, mla_q_norm, mla_w_uq, mla_kv_norm, mla_w_ukv, mla_qk_gain, dil_qk_gain, ssm_a_re, ssm_a_im, ssm_log_dt, ssm_b_re, ssm_b_im, ssm_c_re, ssm_c_im, ssm_d, ssm_w_glu, ssm_b_glu, mix_norm, w_out, norm2, ffn_w_up, ffn_conv_w, ffn_conv_b, ffn_w_down):
    bsz, seq, d = x.shape
    depth = w_mod.shape[0]
    assert d == D_MODEL and seq % TOKEN_TILE == 0 and seq % SSM_CHUNK == 0
    tabs = _rope_tables(positions)
    mod = _modulation(c, w_mod, b_mod)
    inproj_params = jax.vmap(_prep_inproj)(w_in, mla_q_norm, mla_w_uq, mla_kv_norm, mla_w_ukv, mla_qk_gain,
                                           dil_qk_gain)
    s5_params = jax.vmap(_s5_matrices)(ssm_a_re, ssm_a_im, ssm_log_dt, ssm_b_re, ssm_b_im, ssm_c_re, ssm_c_im,
                                       ssm_d)
    for l in range(depth):
        sh1, sc1, g1, sh2, sc2, g2 = (m.reshape(bsz, 1, d) for m in jnp.split(mod[l], N_MOD, axis=-1))
        qa, ka, va, qb, kb, vb, *uc = _inproj(x, sc1, sh1, norm1[l].reshape(1, d), *(p[l] for p in inproj_params),
                                             tabs)
        oa = _attention(qa, ka, va)
        ob = _dilated_attention(qb, kb, vb)
        wz, mm, vt, at = (p[l] for p in s5_params)
        y = _s5_unpack(_s5(_s5_pack(uc), wz, mm, vt, at, bsz), bsz)
        x = _mix(oa, ob, y, x, g1, ssm_w_glu[l].astype(BF16), ssm_b_glu[l].reshape(1, -1),
                 mix_norm[l].reshape(1, -1), w_out[l].astype(BF16))
        x = _ffn(x, sc2, sh2, g2, norm2[l].reshape(1, d), ffn_w_up[l].astype(BF16), ffn_conv_w[l],
                 ffn_conv_b[l].reshape(1, -1), ffn_w_down[l].astype(BF16))
    return x
```

```python
import functools
import math

import jax
import jax.numpy as jnp
from jax import lax
from jax.experimental import pallas as pl
from jax.experimental.pallas import tpu as pltpu

F32 = jnp.float32
BF16 = jnp.bfloat16

D_MODEL = 1024
MLA_HEADS = 6
MLA_NOPE = 64
MLA_ROPE = 32
MLA_V = 64
MLA_QK = MLA_NOPE + MLA_ROPE
MLA_Q_RANK = 192
MLA_KV_RANK = 128
DIL_HEADS = 6
DIL_HEAD_DIM = 64
DIL_PATTERNS = ((128, 1), (512, 4), (2048, 16))
SSM_GROUPS = 16
SSM_GROUP_CH = 16
SSM_CH = SSM_GROUPS * SSM_GROUP_CH
SSM_STATE = 64
WIDTH_A = MLA_HEADS * MLA_V
WIDTH_B = DIL_HEADS * DIL_HEAD_DIM
MIX_WIDTH = WIDTH_A + WIDTH_B + SSM_CH
IN_A = MLA_Q_RANK + MLA_KV_RANK + MLA_ROPE
IN_B = 3 * WIDTH_B
FFN_HIDDEN = 2816
ROPE_THETA = 10000.0
EPS = 1e-6
NEG_INF = -1e30
N_MOD = 6

LANE = 128
SUBLANE = 8
HALF = LANE // 2
Q_RANK_PAD = 256
TOKEN_TILE = 512
FFN_TILE = 1024
ATTN_Q_TILE = 1024
ATTN_PV_SPLIT = 2
DIL_Q_BLOCK = 128
DIL_GROUP = 4
FFN_CHUNK = 256
SSM_CHUNK = 32
VMEM_LIMIT = 56 * 1024 * 1024

C_CQ = 0
C_CKV = C_CQ + Q_RANK_PAD
C_KR = C_CKV + MLA_KV_RANK
C_KRS = C_KR + LANE
C_DQ = C_KRS + LANE
C_DQS = C_DQ + WIDTH_B
C_DK = C_DQS + WIDTH_B
C_DKS = C_DK + WIDTH_B
C_DV = C_DKS + WIDTH_B
C_UC = C_DV + WIDTH_B
IN_EXT = C_UC + SSM_CH
N_PAIRS = MLA_HEADS // 2
QA_W = MLA_HEADS * LANE


def _cparams(*sem):
    return pltpu.CompilerParams(dimension_semantics=sem, vmem_limit_bytes=VMEM_LIMIT)


def _resident(shape):
    nd = len(shape)
    return pl.BlockSpec(shape, lambda *_: (0,) * nd, pipeline_mode=pl.Buffered(1))


def _rms_scale(v, width):
    return lax.rsqrt(jnp.sum(v * v, axis=-1, keepdims=True) * (1.0 / width) + EPS)


def _bdot(a, b):
    return jnp.dot(a.astype(BF16), b, preferred_element_type=F32)


def _mod_kernel(c_ref, w_ref, b_ref, o_ref):
    c = c_ref[...]
    ca = c * (1.0 / (1.0 + jnp.exp(-c)))
    w = w_ref[...]
    c_hi = ca.astype(BF16)
    c_lo = (ca - c_hi.astype(F32)).astype(BF16)
    w_hi = w.astype(BF16)
    w_lo = (w - w_hi.astype(F32)).astype(BF16)
    acc = jnp.dot(c_hi, w_hi, preferred_element_type=F32)
    acc += jnp.dot(c_lo, w_hi, preferred_element_type=F32)
    acc += jnp.dot(c_hi, w_lo, preferred_element_type=F32)
    o_ref[...] = acc + b_ref[...]


def _modulation(c, w_mod, b_mod):
    depth, d, n = w_mod.shape
    bsz = c.shape[0]
    tn = 1024
    return pl.pallas_call(
        _mod_kernel,
        grid=(depth, n // tn),
        in_specs=[
            pl.BlockSpec((bsz, d), lambda l, j: (0, 0)),
            pl.BlockSpec((None, d, tn), lambda l, j: (l, 0, j)),
            pl.BlockSpec((None, 1, tn), lambda l, j: (l, 0, j)),
        ],
        out_specs=pl.BlockSpec((None, bsz, tn), lambda l, j: (l, 0, j)),
        out_shape=jax.ShapeDtypeStruct((depth, bsz, n), F32),
        compiler_params=_cparams("parallel", "parallel"),
        name="modulation",
    )(c, w_mod, b_mod.reshape(depth, 1, n))


def _inproj_kernel(x_ref, sc_ref, sh_ref, n1_ref, win_ref, qg_ref, wuq_ref, kvg_ref, wukv_ref,
                   ag_ref, bg_ref, ca_ref, sa_ref, cb_ref, sb_ref,
                   qa_ref, ka_ref, va_ref, qb_ref, kb_ref, vb_ref, *uc_refs):
    x = x_ref[...]
    h = x * _rms_scale(x, D_MODEL) * n1_ref[...]
    h = h * (1.0 + sc_ref[...]) + sh_ref[...]
    u = _bdot(h, win_ref[...])

    cq = u[:, C_CQ:C_CQ + Q_RANK_PAD]
    cqn = cq * _rms_scale(cq, MLA_Q_RANK) * qg_ref[...]
    q_all = _bdot(cqn, wuq_ref[...])
    ckv = u[:, C_CKV:C_CKV + MLA_KV_RANK]
    ckvn = ckv * _rms_scale(ckv, MLA_KV_RANK) * kvg_ref[...]
    kv_all = _bdot(ckvn, wukv_ref[...])
    kr = u[:, C_KR:C_KR + LANE]
    krs = u[:, C_KRS:C_KRS + LANE]

    ca, sa = ca_ref[...], sa_ref[...]
    gq, gqs, gk, gks = ag_ref[0:1, :], ag_ref[1:2, :], ag_ref[2:3, :], ag_ref[3:4, :]
    cos_q, sin_q = ca * gq, sa * gqs
    cos_k, sin_k = ca * gk, sa * gks
    krs_rot = krs * sin_k
    a_scale = MLA_QK ** -0.5
    for hd in range(MLA_HEADS):
        q = q_all[:, hd * LANE:(hd + 1) * LANE]
        qs = q_all[:, QA_W + hd * LANE:QA_W + (hd + 1) * LANE]
        r = _rms_scale(q, MLA_QK) * a_scale
        qa_ref[:, hd * LANE:(hd + 1) * LANE] = ((q * cos_q + qs * sin_q) * r).astype(BF16)
        k = kv_all[:, hd * LANE:(hd + 1) * LANE] + kr
        r = _rms_scale(k, MLA_QK)
        ka_ref[:, hd * LANE:(hd + 1) * LANE] = ((k * cos_k + krs_rot) * r).astype(BF16)
    va_ref[...] = kv_all[:, QA_W:QA_W + WIDTH_A].astype(BF16)

    cb, sb = cb_ref[...], sb_ref[...]
    gq, gqs, gk, gks = bg_ref[0:1, :], bg_ref[1:2, :], bg_ref[2:3, :], bg_ref[3:4, :]
    cos_q, sin_q = cb * gq, sb * gqs
    cos_k, sin_k = cb * gk, sb * gks
    low = lax.broadcasted_iota(jnp.int32, (x.shape[0], LANE), 1) < HALF
    b_scale = DIL_HEAD_DIM ** -0.5

    def pair_scale(v):
        sq = v * v
        s_all = jnp.sum(sq, axis=-1, keepdims=True)
        s_low = jnp.sum(jnp.where(low, sq, 0.0), axis=-1, keepdims=True)
        r_low = lax.rsqrt(s_low * (1.0 / DIL_HEAD_DIM) + EPS)
        r_high = lax.rsqrt((s_all - s_low) * (1.0 / DIL_HEAD_DIM) + EPS)
        return jnp.where(low, r_low, r_high)

    for p in range(N_PAIRS):
        q = u[:, C_DQ + p * LANE:C_DQ + (p + 1) * LANE]
        qs = u[:, C_DQS + p * LANE:C_DQS + (p + 1) * LANE]
        qo = (q * cos_q + qs * sin_q) * (pair_scale(q) * b_scale)
        qb_ref[:, 2 * p * LANE:(2 * p + 1) * LANE] = jnp.where(low, qo, 0.0).astype(BF16)
        qb_ref[:, (2 * p + 1) * LANE:(2 * p + 2) * LANE] = jnp.where(low, 0.0, qo).astype(BF16)
        k = u[:, C_DK + p * LANE:C_DK + (p + 1) * LANE]
        ks = u[:, C_DKS + p * LANE:C_DKS + (p + 1) * LANE]
        kb_ref[:, p * LANE:(p + 1) * LANE] = ((k * cos_k + ks * sin_k) * pair_scale(k)).astype(BF16)
    vb_ref[...] = u[:, C_DV:C_DV + WIDTH_B].astype(BF16)
    for i, uc_ref in enumerate(uc_refs):
        uc_ref[...] = u[:, C_UC + i * LANE:C_UC + (i + 1) * LANE]


def _inproj(x, sc1, sh1, n1, win, qg, wuq, kvg, wukv, a_gains, b_gains, tabs):
    bsz, seq, d = x.shape
    tm = TOKEN_TILE
    tok = lambda w: pl.BlockSpec((None, tm, w), lambda b, t: (b, t, 0))
    per_b = pl.BlockSpec((None, 1, d), lambda b, t: (b, 0, 0))
    outs = [(QA_W, BF16), (QA_W, BF16), (WIDTH_A, BF16), (2 * WIDTH_B, BF16), (WIDTH_B, BF16),
            (WIDTH_B, BF16)] + [(LANE, F32)] * (SSM_CH // LANE)
    return pl.pallas_call(
        _inproj_kernel,
        grid=(bsz, seq // tm),
        in_specs=[tok(d), per_b, per_b, _resident((1, d)), _resident(win.shape), _resident(qg.shape),
                  _resident(wuq.shape), _resident(kvg.shape), _resident(wukv.shape),
                  _resident(a_gains.shape), _resident(b_gains.shape),
                  tok(LANE), tok(LANE), tok(LANE), tok(LANE)],
        out_specs=[tok(w) for w, _ in outs],
        out_shape=[jax.ShapeDtypeStruct((bsz, seq, w), dt) for w, dt in outs],
        compiler_params=_cparams("parallel", "parallel"),
        name="inproj",
    )(x, sc1, sh1, n1, win, qg, wuq, kvg, wukv, a_gains, b_gains, *tabs)


def _attn_kernel(q_ref, k_ref, v_ref, o_ref):
    v = v_ref[...]
    low_k = lax.broadcasted_iota(jnp.int32, v.shape, 1) < HALF
    outs = []
    for j in range(2):
        q = q_ref[:, j * LANE:(j + 1) * LANE]
        k = k_ref[:, j * LANE:(j + 1) * LANE]
        s = lax.dot_general(q, k, (((1,), (1,)), ((), ())), preferred_element_type=F32)
        m = jnp.max(s, axis=-1, keepdims=True)
        pb = jnp.exp(s - m).astype(BF16)
        vb = jnp.where(low_k == (j == 0), v, jnp.ones_like(v))
        rows = pb.shape[0] // ATTN_PV_SPLIT
        o = jnp.concatenate([jnp.dot(pb[i * rows:(i + 1) * rows], vb, preferred_element_type=F32)
                             for i in range(ATTN_PV_SPLIT)], axis=0)
        outs.append(o * (1.0 / pltpu.roll(o, HALF, axis=1)))
    low = lax.broadcasted_iota(jnp.int32, outs[0].shape, 1) < HALF
    o_ref[...] = jnp.where(low, outs[0], outs[1]).astype(o_ref.dtype)


def _attention(q, k, v):
    bsz, seq, _ = q.shape
    tq = ATTN_Q_TILE
    return pl.pallas_call(
        _attn_kernel,
        grid=(bsz, N_PAIRS, seq // tq),
        in_specs=[
            pl.BlockSpec((None, tq, 2 * LANE), lambda b, p, i: (b, i, p)),
            pl.BlockSpec((None, seq, 2 * LANE), lambda b, p, i: (b, 0, p)),
            pl.BlockSpec((None, seq, LANE), lambda b, p, i: (b, 0, p)),
        ],
        out_specs=pl.BlockSpec((None, tq, LANE), lambda b, p, i: (b, i, p)),
        out_shape=jax.ShapeDtypeStruct((bsz, seq, N_PAIRS * LANE), BF16),
        compiler_params=_cparams("parallel", "parallel", "parallel"),
        name="mla_attention",
    )(q, k, v)


def _dilated_kernel(q_ref, k_ref, v_ref, o_ref, qf, kf, vf, acc, mx):
    seq = k_ref.shape[0]
    kf[...] = k_ref[...].astype(F32)
    vf[...] = v_ref[...].astype(F32)
    for j in range(2):
        qf[j] = q_ref[:, j * LANE:(j + 1) * LANE].astype(F32)
    qb = DIL_Q_BLOCK
    for branch, (window, dil) in enumerate(DIL_PATTERNS):
        half = window // (2 * dil)
        n_sub = seq // dil
        kw = min(n_sub, qb + 2 * half)
        key_minus_query = (lax.broadcasted_iota(jnp.int32, (qb, kw), 1)
                           - lax.broadcasted_iota(jnp.int32, (qb, kw), 0))
        low_k = lax.broadcasted_iota(jnp.int32, (kw, LANE), 1) < HALF
        blocks = [(res, j0) for res in range(dil) for j0 in range(0, n_sub, qb)]
        for g0 in range(0, len(blocks), DIL_GROUP):
            chains = []
            for res, j0 in blocks[g0:g0 + DIL_GROUP]:
                k0 = min(max(j0 - half, 0), n_sub - kw)
                band = jnp.where(jnp.abs(key_minus_query + (k0 - j0)) <= half, 0.0, NEG_INF)
                q_rows = pl.ds(res + dil * j0, qb, stride=dil)
                k_rows = pl.ds(res + dil * k0, kw, stride=dil)
                kb = kf[k_rows, :].astype(BF16)
                v_win = vf[k_rows, :]
                for j in range(2):
                    vb = jnp.where(low_k == (j == 0), v_win, 1.0).astype(BF16)
                    q = qf[j, q_rows, :].astype(BF16)
                    s = lax.dot_general(q, kb, (((1,), (1,)), ((), ())), preferred_element_type=F32) + band
                    chains.append((j, q_rows, s, vb))
            probs = []
            for j, q_rows, s, vb in chains:
                m = jnp.max(s, axis=-1, keepdims=True)
                mx[branch, j, q_rows, :] = jnp.broadcast_to(m, (qb, LANE))
                probs.append(jnp.exp(s - m).astype(BF16))
            for (j, q_rows, s, vb), p in zip(chains, probs):
                acc[branch, j, q_rows, :] = jnp.dot(p, vb, preferred_element_type=F32)
    outs = []
    n_br = len(DIL_PATTERNS)
    for j in range(2):
        m_all = mx[0, j]
        for br in range(1, n_br):
            m_all = jnp.maximum(m_all, mx[br, j])
        tot = None
        for br in range(n_br):
            w = jnp.exp(mx[br, j] - m_all)
            tot = w * acc[br, j] if tot is None else tot + w * acc[br, j]
        outs.append(tot * (1.0 / pltpu.roll(tot, HALF, axis=1)))
    low = lax.broadcasted_iota(jnp.int32, (seq, LANE), 1) < HALF
    o_ref[...] = jnp.where(low, outs[0], outs[1]).astype(o_ref.dtype)


def _dilated_attention(q, k, v):
    bsz, seq, _ = q.shape
    assert all(seq % (dil * DIL_Q_BLOCK) == 0 for _, dil in DIL_PATTERNS)
    return pl.pallas_call(
        _dilated_kernel,
        grid=(bsz, N_PAIRS),
        in_specs=[
            pl.BlockSpec((None, seq, 2 * LANE), lambda b, p: (b, 0, p)),
            pl.BlockSpec((None, seq, LANE), lambda b, p: (b, 0, p)),
            pl.BlockSpec((None, seq, LANE), lambda b, p: (b, 0, p)),
        ],
        out_specs=pl.BlockSpec((None, seq, LANE), lambda b, p: (b, 0, p)),
        out_shape=jax.ShapeDtypeStruct((bsz, seq, N_PAIRS * LANE), BF16),
        scratch_shapes=[pltpu.VMEM((2, seq, LANE), F32), pltpu.VMEM((seq, LANE), F32),
                        pltpu.VMEM((seq, LANE), F32)]
        + [pltpu.VMEM((len(DIL_PATTERNS), 2, seq, LANE), F32)] * 2,
        compiler_params=_cparams("parallel", "parallel"),
        name="dilated_attention",
    )(q, k, v)


def _s5_kernel(u_ref, wz_ref, m_ref, vt_ref, at_ref, y_ref, z_scr, s_scr, *, n_chunks, bsz):
    u = u_ref[...]
    z = jnp.dot(u, wz_ref[...], preferred_element_type=F32)
    for slab in range(4):
        z_scr[slab] = z[:, slab * LANE:(slab + 1) * LANE]
    at = at_ref[...]
    ar_f, ai_f, ar_b, ai_b = at[0:1, :], at[1:2, :], at[2:3, :], at[3:4, :]

    def chunk_rows(kk):
        return pl.ds(kk, bsz, stride=n_chunks)

    def step(i, carry):
        re_f, im_f, re_b, im_b = carry
        kf, kb = chunk_rows(i), chunk_rows(n_chunks - 1 - i)
        s_scr[0, kf, :] = re_f
        s_scr[1, kf, :] = im_f
        s_scr[2, kb, :] = re_b
        s_scr[3, kb, :] = im_b
        return (ar_f * re_f - ai_f * im_f + z_scr[0, kf, :], ar_f * im_f + ai_f * re_f + z_scr[1, kf, :],
                ar_b * re_b - ai_b * im_b + z_scr[2, kb, :], ar_b * im_b + ai_b * re_b + z_scr[3, kb, :])

    zero = jnp.zeros((bsz, LANE), F32)
    lax.fori_loop(0, n_chunks, step, (zero, zero, zero, zero), unroll=2)
    y = jnp.dot(u, m_ref[...], preferred_element_type=F32)
    states = jnp.concatenate([s_scr[slab] for slab in range(4)], axis=1).astype(BF16)
    y += lax.dot_general(states, vt_ref[...], (((1,), (1,)), ((), ())), preferred_element_type=F32)
    y_ref[...] = y


def _s5(u_g, wz, mm, vt, at, bsz):
    groups, rows, width = u_g.shape
    n_chunks = rows // bsz
    sw = 4 * LANE
    return pl.pallas_call(
        functools.partial(_s5_kernel, n_chunks=n_chunks, bsz=bsz),
        grid=(groups,),
        in_specs=[
            pl.BlockSpec((None, rows, width), lambda g: (g, 0, 0)),
            pl.BlockSpec((None, width, sw), lambda g: (g, 0, 0)),
            pl.BlockSpec((None, width, width), lambda g: (g, 0, 0)),
            pl.BlockSpec((None, width, sw), lambda g: (g, 0, 0)),
            pl.BlockSpec((None, SUBLANE, LANE), lambda g: (g, 0, 0)),
        ],
        out_specs=pl.BlockSpec((None, rows, width), lambda g: (g, 0, 0)),
        out_shape=jax.ShapeDtypeStruct((groups, rows, width), F32),
        scratch_shapes=[pltpu.VMEM((4, rows, LANE), F32), pltpu.VMEM((4, rows, LANE), F32)],
        compiler_params=_cparams("parallel"),
        name="s5",
    )(u_g, wz, mm, vt, at)


PIECE = SSM_GROUP_CH
PIECES = LANE // PIECE
GROUP_TILES = SSM_CH // LANE


def _piece_gather(load_tile, shifts_and_tiles):
    acc = None
    for slot, (tile_idx, shift) in enumerate(shifts_and_tiles):
        src = load_tile(tile_idx)
        if shift % LANE:
            src = pltpu.roll(src, shift % LANE, axis=1)
        if acc is None:
            acc = src
        else:
            piece = lax.broadcasted_iota(jnp.int32, src.shape, 1) // PIECE
            acc = jnp.where(piece == slot, src, acc)
    return acc


def _s5_pack_kernel(*refs):
    z_refs, o_ref = refs[:GROUP_TILES], refs[GROUP_TILES]
    n_chunks = o_ref.shape[1]

    tiles = [z_refs[i % GROUP_TILES][pl.ds(i // GROUP_TILES, n_chunks, stride=SSM_CHUNK), :]
             for i in range(SSM_CHUNK * GROUP_TILES)]
    for g in range(SSM_GROUPS):
        q = g % PIECES
        for jt in range(SSM_CHUNK // PIECES):
            plan = [((jt * PIECES + jj) * GROUP_TILES + g // PIECES, (jj - q) * PIECE) for jj in range(PIECES)]
            o_ref[g, :, jt * LANE:(jt + 1) * LANE] = _piece_gather(lambda i: tiles[i], plan).astype(BF16)


def _s5_unpack_kernel(y_ref, o_ref):
    for j in range(SSM_CHUNK):
        jj = j % PIECES
        jt = j // PIECES
        for half in range(GROUP_TILES):
            plan = [((half * PIECES + q), (q - jj) * PIECE) for q in range(PIECES)]
            tile = _piece_gather(lambda g: y_ref[g, :, jt * LANE:(jt + 1) * LANE], plan)
            o_ref[:, (j * GROUP_TILES + half) * LANE:(j * GROUP_TILES + half + 1) * LANE] = tile.astype(BF16)


def _s5_pack(uc_halves):
    bsz, seq, _ = uc_halves[0].shape
    n_chunks = seq // SSM_CHUNK
    width = SSM_CHUNK * SSM_GROUP_CH
    return pl.pallas_call(
        _s5_pack_kernel,
        grid=(bsz,),
        in_specs=[pl.BlockSpec((None, seq, LANE), lambda b: (b, 0, 0))] * GROUP_TILES,
        out_specs=pl.BlockSpec((SSM_GROUPS, n_chunks, width), lambda b: (0, b, 0)),
        out_shape=jax.ShapeDtypeStruct((SSM_GROUPS, bsz * n_chunks, width), BF16),
        compiler_params=_cparams("parallel"),
        name="s5_pack",
    )(*uc_halves)


def _s5_unpack(y_g, bsz):
    groups, rows, width = y_g.shape
    n_chunks = rows // bsz
    out = pl.pallas_call(
        _s5_unpack_kernel,
        grid=(bsz,),
        in_specs=[pl.BlockSpec((groups, n_chunks, width), lambda b: (0, b, 0))],
        out_specs=pl.BlockSpec((None, n_chunks, SSM_CHUNK * SSM_CH), lambda b: (b, 0, 0)),
        out_shape=jax.ShapeDtypeStruct((bsz, n_chunks, SSM_CHUNK * SSM_CH), BF16),
        compiler_params=_cparams("parallel"),
        name="s5_unpack",
    )(y_g)
    return out.reshape(bsz, n_chunks * SSM_CHUNK, SSM_CH)


def _s5_matrices(a_re, a_im, log_dt, b_re, b_im, c_re, c_im, d_skip):
    t = SSM_CHUNK
    g, p, cg = SSM_GROUPS, SSM_STATE, SSM_GROUP_CH
    a = lax.complex(a_re.astype(F32), a_im.astype(F32))
    dt = jnp.exp(log_dt.astype(F32))[..., None]
    lam = a * dt
    a_bar = jnp.exp(lam)
    b_bar = ((a_bar - 1.0) / a)[..., None] * lax.complex(b_re.astype(F32), b_im.astype(F32))
    c = lax.complex(c_re.astype(F32), c_im.astype(F32))
    n = jnp.arange(t + 1, dtype=F32)
    pw = jnp.exp(lam[:, None] * n[None, :, None, None].astype(jnp.complex64))
    kern = jnp.einsum("dgcp,dtgp,dgpe->dtgce", c, pw[:, :t], b_bar).real
    eye = jnp.eye(cg, dtype=F32) * d_skip.astype(F32).reshape(g, cg)[:, :, None]
    lags = jnp.concatenate([kern[1][1:][::-1], (kern[0][0] + kern[1][0] + eye)[None], kern[0][1:]], 0)
    strip = lags.transpose(1, 3, 0, 2).reshape(g, cg, (2 * t - 1) * cg)
    mm = jnp.stack([strip[:, :, (t - 1 - j) * cg:(2 * t - 1 - j) * cg] for j in range(t)], axis=1)
    mm = mm.reshape(g, t * cg, t * cg)
    def lane_padded(v):
        return jnp.pad(v, [(0, 0)] * (v.ndim - 1) + [(0, LANE - p)])

    pw_g = pw.transpose(0, 2, 1, 3)
    pw_re, pw_im = lane_padded(pw_g.real), lane_padded(pw_g.imag)
    b_t = b_bar.transpose(0, 1, 3, 2)
    bt_re, bt_im = lane_padded(b_t.real), lane_padded(b_t.imag)
    c_re_p, c_im_p = lane_padded(c.real), lane_padded(c.imag)

    def slabs(a_re_, a_im_, m_re, m_im, sign):
        a_re_, a_im_ = a_re_[:, :, None, :], a_im_[:, :, None, :]
        m_re, m_im = m_re[:, None], m_im[:, None]
        re = a_re_ * m_re - a_im_ * m_im
        im = a_re_ * m_im + a_im_ * m_re
        return [re.reshape(g, t * cg, LANE), (sign * im).reshape(g, t * cg, LANE)]

    wz = jnp.concatenate(
        slabs(pw_re[0, :, :t][:, ::-1], pw_im[0, :, :t][:, ::-1], bt_re[0], bt_im[0], 1.0)
        + slabs(pw_re[1, :, :t], pw_im[1, :, :t], bt_re[1], bt_im[1], 1.0), axis=-1)
    vt = jnp.concatenate(
        slabs(pw_re[0, :, 1:t + 1], pw_im[0, :, 1:t + 1], c_re_p[0], c_im_p[0], -1.0)
        + slabs(pw_re[1, :, 1:t + 1][:, ::-1], pw_im[1, :, 1:t + 1][:, ::-1], c_re_p[1], c_im_p[1], -1.0),
        axis=-1)
    at_c = pw[:, t]
    lane_pad = jnp.zeros((g, LANE - p), F32)
    rows = [jnp.concatenate([at_c[0].real, lane_pad], -1), jnp.concatenate([at_c[0].imag, lane_pad], -1),
            jnp.concatenate([at_c[1].real, lane_pad], -1), jnp.concatenate([at_c[1].imag, lane_pad], -1)]
    at = jnp.stack(rows + [jnp.zeros((g, LANE), F32)] * (SUBLANE - 4), axis=1)
    return wz.astype(BF16), mm.astype(BF16), vt.astype(BF16), at


def _mix_kernel(oa_ref, ob_ref, y_ref, x_ref, g1_ref, wglu_ref, bglu_ref, gm_ref, wo_ref, o_ref):
    y = y_ref[...].astype(F32)
    y = 0.5 * y * (1.0 + jnp.tanh(math.sqrt(2.0 / math.pi) * (y + 0.044715 * (y * y * y))))
    z = _bdot(y, wglu_ref[...]) + bglu_ref[...]
    gate = z[:, SSM_CH:]
    oc = z[:, :SSM_CH] * (1.0 / (1.0 + jnp.exp(-gate)))
    oa, ob = oa_ref[...].astype(F32), ob_ref[...].astype(F32)
    na = oa * _rms_scale(oa, WIDTH_A) * gm_ref[:, 0:WIDTH_A]
    nb = ob * _rms_scale(ob, WIDTH_B) * gm_ref[:, WIDTH_A:WIDTH_A + WIDTH_B]
    nc = oc * _rms_scale(oc, SSM_CH) * gm_ref[:, WIDTH_A + WIDTH_B:]
    mixed = jnp.concatenate([na.astype(BF16), nb.astype(BF16), nc.astype(BF16)], axis=1)
    acc = jnp.dot(mixed, wo_ref[...], preferred_element_type=F32)
    o_ref[...] = x_ref[...] + g1_ref[...] * acc


def _mix(oa, ob, y, x, g1, wglu, bglu, gm, wo):
    bsz, seq, d = x.shape
    tm = TOKEN_TILE
    tok = lambda w: pl.BlockSpec((None, tm, w), lambda b, t: (b, t, 0))
    per_b = pl.BlockSpec((None, 1, d), lambda b, t: (b, 0, 0))
    return pl.pallas_call(
        _mix_kernel,
        grid=(bsz, seq // tm),
        in_specs=[tok(WIDTH_A), tok(WIDTH_B), tok(SSM_CH), tok(d), per_b, _resident(wglu.shape),
                  _resident(bglu.shape), _resident(gm.shape), _resident(wo.shape)],
        out_specs=tok(d),
        out_shape=jax.ShapeDtypeStruct((bsz, seq, d), F32),
        compiler_params=_cparams("parallel", "parallel"),
        name="mix",
    )(oa, ob, y, x, g1, wglu, bglu, gm, wo)


def _ffn_kernel(x_ref, xp_ref, xn_ref, sc_ref, sh_ref, g2_ref, n2_ref, wup_ref, cw_ref, cb_ref, wdn_ref,
                o_ref, h_scr, z_scr, act_scr, *, n_tiles):
    tm = x_ref.shape[0]
    t = pl.program_id(1)
    mod_scale = n2_ref[...] * (1.0 + sc_ref[...])
    shift = sh_ref[...]

    def normed(v):
        return v * _rms_scale(v, D_MODEL) * mod_scale + shift

    x = x_ref[...]
    not_first = (t > 0).astype(F32)
    not_last = (t < n_tiles - 1).astype(F32)
    h_scr[0:SUBLANE, :] = normed(xp_ref[...]) * not_first
    h_scr[SUBLANE:SUBLANE + tm, :] = normed(x)
    h_scr[SUBLANE + tm:, :] = normed(xn_ref[...]) * not_last
    h = h_scr[...].astype(BF16)
    fc = FFN_CHUNK

    def conv_cols(slot, col0):
        z_scr[slot] = jnp.dot(h, wup_ref[:, col0:col0 + fc], preferred_element_type=F32)
        cw = cw_ref[:, col0:col0 + fc]
        prev = z_scr[slot, SUBLANE - 1:SUBLANE - 1 + tm, :]
        cur = z_scr[slot, SUBLANE:SUBLANE + tm, :]
        nxt = z_scr[slot, SUBLANE + 1:SUBLANE + 1 + tm, :]
        return prev * cw[0:1, :] + cur * cw[1:2, :] + nxt * cw[2:3, :] + cb_ref[:, col0:col0 + fc]

    for f in range(FFN_HIDDEN // fc):
        val = conv_cols(0, f * fc)
        gate = conv_cols(1, FFN_HIDDEN + f * fc)
        act_scr[:, f * fc:(f + 1) * fc] = (gate * (1.0 / (1.0 + jnp.exp(-gate))) * val).astype(BF16)
    down = jnp.dot(act_scr[...], wdn_ref[...], preferred_element_type=F32)
    o_ref[...] = x + g2_ref[...] * down


def _ffn(x, sc2, sh2, g2, n2, wup, cw, cb, wdn):
    bsz, seq, d = x.shape
    tm = FFN_TILE
    n_tiles = seq // tm
    rows8 = tm // SUBLANE
    tok = pl.BlockSpec((None, tm, d), lambda b, t: (b, t, 0))
    prev8 = pl.BlockSpec((None, SUBLANE, d), lambda b, t: (b, jnp.maximum(t * rows8 - 1, 0), 0))
    next8 = pl.BlockSpec((None, SUBLANE, d),
                         lambda b, t: (b, jnp.minimum((t + 1) * rows8, seq // SUBLANE - 1), 0))
    per_b = pl.BlockSpec((None, 1, d), lambda b, t: (b, 0, 0))
    return pl.pallas_call(
        functools.partial(_ffn_kernel, n_tiles=n_tiles),
        grid=(bsz, n_tiles),
        in_specs=[tok, prev8, next8, per_b, per_b, per_b, _resident(n2.shape), _resident(wup.shape),
                  _resident(cw.shape), _resident(cb.shape), _resident(wdn.shape)],
        out_specs=tok,
        out_shape=jax.ShapeDtypeStruct((bsz, seq, d), F32),
        scratch_shapes=[pltpu.VMEM((tm + 2 * SUBLANE, d), F32),
                        pltpu.VMEM((2, tm + 2 * SUBLANE, FFN_CHUNK), F32),
                        pltpu.VMEM((tm, FFN_HIDDEN), BF16)],
        compiler_params=_cparams("parallel", "parallel"),
        name="ffn",
    )(x, x, x, sc2, sh2, g2, n2, wup, cw, cb, wdn)


def _swap_halves(w, heads, dim):
    half = dim // 2
    parts = []
    for hd in range(heads):
        parts += [w[..., hd * dim + half:(hd + 1) * dim], w[..., hd * dim:hd * dim + half]]
    return jnp.concatenate(parts, axis=-1)


def _prep_inproj(w_in, q_gain, w_uq, kv_gain, w_ukv, mla_gain, dil_gain):
    d = w_in.shape[0]
    z = lambda n: jnp.zeros((d, n), F32)
    a, b = w_in[:, :IN_A], w_in[:, IN_A:IN_A + IN_B]
    rope = a[:, MLA_Q_RANK + MLA_KV_RANK:]
    dq, dk, dv = b[:, :WIDTH_B], b[:, WIDTH_B:2 * WIDTH_B], b[:, 2 * WIDTH_B:]
    win = jnp.concatenate([
        a[:, :MLA_Q_RANK], z(Q_RANK_PAD - MLA_Q_RANK),
        a[:, MLA_Q_RANK:MLA_Q_RANK + MLA_KV_RANK],
        z(MLA_NOPE), rope, z(LANE - MLA_QK),
        z(MLA_NOPE), _swap_halves(rope, 1, MLA_ROPE), z(LANE - MLA_QK),
        dq, _swap_halves(dq, DIL_HEADS, DIL_HEAD_DIM),
        dk, _swap_halves(dk, DIL_HEADS, DIL_HEAD_DIM),
        dv, w_in[:, IN_A + IN_B:]], axis=1).astype(BF16)

    wq = w_uq.reshape(MLA_Q_RANK, MLA_HEADS, MLA_QK)
    zq = lambda n: jnp.zeros((MLA_Q_RANK, MLA_HEADS, n), F32)
    slab = jnp.concatenate([wq, zq(LANE - MLA_QK)], -1)
    slab_s = jnp.concatenate([zq(MLA_NOPE), wq[..., MLA_NOPE + MLA_ROPE // 2:],
                              wq[..., MLA_NOPE:MLA_NOPE + MLA_ROPE // 2], zq(LANE - MLA_QK)], -1)
    wuq = jnp.concatenate([slab.reshape(MLA_Q_RANK, QA_W), slab_s.reshape(MLA_Q_RANK, QA_W)], 1)
    wuq = jnp.pad(wuq, ((0, Q_RANK_PAD - MLA_Q_RANK), (0, 0))).astype(BF16)
    qg = jnp.pad(q_gain, (0, Q_RANK_PAD - MLA_Q_RANK)).reshape(1, Q_RANK_PAD)

    wkv = w_ukv.reshape(MLA_KV_RANK, MLA_HEADS, MLA_NOPE + MLA_V)
    k_slab = jnp.concatenate([wkv[..., :MLA_NOPE], jnp.zeros((MLA_KV_RANK, MLA_HEADS, LANE - MLA_NOPE), F32)], -1)
    wukv = jnp.concatenate([k_slab.reshape(MLA_KV_RANK, QA_W),
                            wkv[..., MLA_NOPE:].reshape(MLA_KV_RANK, WIDTH_A)], 1).astype(BF16)
    kvg = kv_gain.reshape(1, MLA_KV_RANK)

    def mla_rows(g):
        pad = jnp.zeros((LANE - MLA_QK,), F32)
        plain = jnp.concatenate([g, pad])
        swapped = jnp.concatenate([jnp.zeros((MLA_NOPE,), F32), g[MLA_NOPE + MLA_ROPE // 2:],
                                   g[MLA_NOPE:MLA_NOPE + MLA_ROPE // 2], pad])
        return [plain, swapped]

    def dil_rows(g):
        return [jnp.tile(g, 2), jnp.tile(_swap_halves(g, 1, DIL_HEAD_DIM), 2)]

    fill = [jnp.zeros((LANE,), F32)] * (SUBLANE - 4)
    a_gains = jnp.stack(mla_rows(mla_gain[0]) + mla_rows(mla_gain[1]) + fill)
    b_gains = jnp.stack(dil_rows(dil_gain[0]) + dil_rows(dil_gain[1]) + fill)
    return win, qg, wuq, kvg, wukv, a_gains, b_gains


def _rope_tables(positions):
    def tables(dim):
        inv_freq = 1.0 / (ROPE_THETA ** (jnp.arange(0, dim, 2, dtype=F32) / dim))
        ang = positions.astype(F32)[..., None] * inv_freq
        return jnp.cos(ang), jnp.sin(ang)

    cos_r, sin_r, cos_f, sin_f = lax.optimization_barrier(tables(MLA_ROPE) + tables(DIL_HEAD_DIM))
    lead = positions.shape
    ones = jnp.ones(lead + (MLA_NOPE,), F32)
    zeros = jnp.zeros(lead + (MLA_NOPE,), F32)
    pad = jnp.zeros(lead + (LANE - MLA_QK,), F32)
    ca = jnp.concatenate([ones, cos_r, cos_r, pad], -1)
    sa = jnp.concatenate([zeros, -sin_r, sin_r, pad], -1)
    cb = jnp.concatenate([cos_f, cos_f, cos_f, cos_f], -1)
    sb = jnp.concatenate([-sin_f, sin_f, -sin_f, sin_f], -1)
    return ca, sa, cb, sb


def kernel(x, c, positions, w_mod, b_mod, norm1, w_in, mla_q_norm, mla_w_uq, mla_kv_norm, mla_w_ukv, mla_qk_gain, dil_qk_gain, ssm_a_re, ssm_a_im, ssm_log_dt, ssm_b_re, ssm_b_im, ssm_c_re, ssm_c_im, ssm_d, ssm_w_glu, ssm_b_glu, mix_norm, w_out, norm2, ffn_w_up, ffn_conv_w, ffn_conv_b, ffn_w_down):
    bsz, seq, d = x.shape
    depth = w_mod.shape[0]
    assert d == D_MODEL and seq % TOKEN_TILE == 0 and seq % SSM_CHUNK == 0
    tabs = _rope_tables(positions)
    mod = _modulation(c, w_mod, b_mod)
    inproj_params = jax.vmap(_prep_inproj)(w_in, mla_q_norm, mla_w_uq, mla_kv_norm, mla_w_ukv, mla_qk_gain,
                                           dil_qk_gain)
    s5_params = jax.vmap(_s5_matrices)(ssm_a_re, ssm_a_im, ssm_log_dt, ssm_b_re, ssm_b_im, ssm_c_re, ssm_c_im,
                                       ssm_d)
    for l in range(depth):
        sh1, sc1, g1, sh2, sc2, g2 = (m.reshape(bsz, 1, d) for m in jnp.split(mod[l], N_MOD, axis=-1))
        qa, ka, va, qb, kb, vb, *uc = _inproj(x, sc1, sh1, norm1[l].reshape(1, d), *(p[l] for p in inproj_params),
                                             tabs)
        oa = _attention(qa, ka, va)
        ob = _dilated_attention(qb, kb, vb)
        wz, mm, vt, at = (p[l] for p in s5_params)
        y = _s5_unpack(_s5(_s5_pack(uc), wz, mm, vt, at, bsz), bsz)
        x = _mix(oa, ob, y, x, g1, ssm_w_glu[l].astype(BF16), ssm_b_glu[l].reshape(1, -1),
                 mix_norm[l].reshape(1, -1), w_out[l].astype(BF16))
        x = _ffn(x, sc2, sh2, g2, norm2[l].reshape(1, d), ffn_w_up[l].astype(BF16), ffn_conv_w[l],
                 ffn_conv_b[l].reshape(1, -1), ffn_w_down[l].astype(BF16))
    return x
```
